```python
import math
import jax, jax.numpy as jnp
from jax import lax
import numpy as np

D_MODEL = 1024
BATCH = 4
SEQ = 4096
DEPTH = 4

N_A_LAYERS = DEPTH // 2
N_B_LAYERS = DEPTH - N_A_LAYERS
QB = 128
DA_HEAD_DIM = 64
DA_HEADS = D_MODEL // (2 * DA_HEAD_DIM)
DA_QK = DA_HEADS * 2 * DA_HEAD_DIM
DA_V = DA_HEADS * 2 * DA_HEAD_DIM
DA_IN = 2 * DA_QK + DA_V + DA_V
LAMBDA_INIT = tuple(0.8 - 0.6 * math.exp(-0.3 * l) for l in range(N_A_LAYERS))
SB_HEAD_DIM = 64
SB_HEADS = D_MODEL // SB_HEAD_DIM
SB_W = SB_HEADS * SB_HEAD_DIM
SB_IN = SB_W + SB_W
ROPE_THETA = 500000.0
ROT_DIM = DA_HEAD_DIM // 4
RMS_EPS = 1e-6

kernel_name = "yoco_diffattn_stickbreaking_hybrid"


def _rms(x, g):
    xf = x.astype(jnp.float32)
    y = xf * lax.rsqrt(jnp.mean(xf * xf, axis=-1, keepdims=True) + RMS_EPS)
    return (y * g.astype(jnp.float32)).astype(x.dtype)


def _rope_partial(t, pos):
    half = ROT_DIM // 2
    inv = jnp.power(jnp.float32(ROPE_THETA), -jnp.arange(half, dtype=jnp.float32) / half)
    ang = pos.astype(jnp.float32)[:, None] * inv[None, :]
    cos = jnp.cos(ang)[None, :, None, :]
    sin = jnp.sin(ang)[None, :, None, :]
    t1 = t[..., :half].astype(jnp.float32)
    t2 = t[..., half:ROT_DIM].astype(jnp.float32)
    rot = jnp.concatenate([t1 * cos - t2 * sin, t1 * sin + t2 * cos], axis=-1)
    return jnp.concatenate([rot.astype(t.dtype), t[..., ROT_DIM:]], axis=-1)


def _to_blocks(t):
    b, s, h, d = t.shape
    return t.reshape(b, s // QB, QB, h, d).transpose(1, 0, 3, 2, 4)


def _from_blocks(o):
    nb, b, h, qb, e = o.shape
    return o.transpose(1, 0, 3, 2, 4).reshape(b, nb * qb, h, e)


def _diff_attention(q1, q2, k1, k2, v, lam):
    s_len = q1.shape[1]
    kpos = jnp.arange(s_len)
    scale = DA_HEAD_DIM ** -0.5

    def block(args):
        i, qb1, qb2 = args
        qpos = i * QB + jnp.arange(QB)
        mask = kpos[None, :] <= qpos[:, None]

        def probs(qb, k):
            s = jnp.einsum('bhqd,bshd->bhqs', qb, k).astype(jnp.float32) * scale
            return jax.nn.softmax(jnp.where(mask, s, -jnp.inf), axis=-1)

        a = probs(qb1, k1) - lam * probs(qb2, k2)
        return jnp.einsum('bhqs,bshe->bhqe', a.astype(v.dtype), v)

    nb = s_len // QB
    o = lax.map(block, (jnp.arange(nb), _to_blocks(q1), _to_blocks(q2)))
    return _from_blocks(o)


def _stick_breaking(q, k, v):
    s_len = q.shape[1]
    kpos = jnp.arange(s_len)
    scale = SB_HEAD_DIM ** -0.5

    def block(args):
        i, qb = args
        qpos = i * QB + jnp.arange(QB)
        mask = kpos[None, :] < qpos[:, None]
        z = jnp.einsum('bhqd,bshd->bhqs', qb, k).astype(jnp.float32) * scale
        log_beta = jax.nn.log_sigmoid(z)
        log_1mb = jnp.where(mask, jax.nn.log_sigmoid(-z), 0.0)
        suffix = lax.cumsum(log_1mb, axis=3, reverse=True) - log_1mb
        a = jnp.where(mask, jnp.exp(log_beta + suffix), 0.0)
        return jnp.einsum('bhqs,bshe->bhqe', a.astype(v.dtype), v)

    nb = s_len // QB
    o = lax.map(block, (jnp.arange(nb), _to_blocks(q)))
    return _from_blocks(o)


def _diff_layer(x, pos, norm_g, w_in, w_out, q_norm, k_norm, lq1, lk1, lq2, lk2,
                subln, lambda_init):
    b, s, _ = x.shape
    h = _rms(x, norm_g)
    proj = h @ w_in
    q, k, v, g = jnp.split(proj, [DA_QK, 2 * DA_QK, 2 * DA_QK + DA_V], axis=-1)
    q = q.reshape(b, s, DA_HEADS, 2, DA_HEAD_DIM)
    k = k.reshape(b, s, DA_HEADS, 2, DA_HEAD_DIM)
    v = v.reshape(b, s, DA_HEADS, 2 * DA_HEAD_DIM)
    qs = [_rope_partial(_rms(q[:, :, :, c], q_norm), pos) for c in range(2)]
    ks = [_rope_partial(_rms(k[:, :, :, c], k_norm), pos) for c in range(2)]
    lam = (jnp.exp(jnp.sum(lq1.astype(jnp.float32) * lk1.astype(jnp.float32)))
           - jnp.exp(jnp.sum(lq2.astype(jnp.float32) * lk2.astype(jnp.float32)))
           + lambda_init)
    o = _diff_attention(qs[0], qs[1], ks[0], ks[1], v, lam)
    o = _rms(o, subln) * (1.0 - lambda_init)
    o = o.reshape(b, s, DA_V) * jax.nn.silu(g)
    return x + o @ w_out


def _shared_kv(x, kv_norm, w_kv):
    b, s, _ = x.shape
    kv = _rms(x, kv_norm) @ w_kv
    k, v = jnp.split(kv, [SB_W], axis=-1)
    return (k.reshape(b, s, SB_HEADS, SB_HEAD_DIM), v.reshape(b, s, SB_HEADS, SB_HEAD_DIM))


def _sb_layer(x, k, v, norm_g, w_in, w_out):
    b, s, _ = x.shape
    proj = _rms(x, norm_g) @ w_in
    q, g = jnp.split(proj, [SB_W], axis=-1)
    o = _stick_breaking(q.reshape(b, s, SB_HEADS, SB_HEAD_DIM), k, v)
    o = o.reshape(b, s, SB_W) * jax.nn.silu(g)
    return x + o @ w_out


def setup_inputs(seed: int = 0) -> dict:
    key = jax.random.key(seed)
    ks = jax.random.split(key, 20)
    f32 = jnp.float32
    nrm = lambda k, shape, scale: jax.random.normal(k, shape, f32) * scale
    gain = lambda k, shape: 1.0 + 0.02 * jax.random.normal(k, shape, f32)
    return {
        "x": jax.random.normal(ks[0], (BATCH, SEQ, D_MODEL), f32),
        "a_norm": gain(ks[1], (N_A_LAYERS, D_MODEL)),
        "a_w_in": nrm(ks[2], (N_A_LAYERS, D_MODEL, DA_IN), D_MODEL ** -0.5),
        "a_w_out": nrm(ks[3], (N_A_LAYERS, DA_V, D_MODEL), DA_V ** -0.5),
        "a_q_norm": gain(ks[4], (N_A_LAYERS, DA_HEAD_DIM)),
        "a_k_norm": gain(ks[5], (N_A_LAYERS, DA_HEAD_DIM)),
        "a_lq1": nrm(ks[6], (N_A_LAYERS, DA_HEAD_DIM), 0.1),
        "a_lk1": nrm(ks[7], (N_A_LAYERS, DA_HEAD_DIM), 0.1),
        "a_lq2": nrm(ks[8], (N_A_LAYERS, DA_HEAD_DIM), 0.1),
        "a_lk2": nrm(ks[9], (N_A_LAYERS, DA_HEAD_DIM), 0.1),
        "a_subln": gain(ks[10], (N_A_LAYERS, 2 * DA_HEAD_DIM)),
        "kv_norm": gain(ks[11], (D_MODEL,)),
        "w_kv": nrm(ks[12], (D_MODEL, 2 * SB_W), D_MODEL ** -0.5),
        "b_norm": gain(ks[13], (N_B_LAYERS, D_MODEL)),
        "b_w_in": nrm(ks[14], (N_B_LAYERS, D_MODEL, SB_IN), D_MODEL ** -0.5),
        "b_w_out": nrm(ks[15], (N_B_LAYERS, SB_W, D_MODEL), SB_W ** -0.5),
    }


def reference(x, a_norm, a_w_in, a_w_out, a_q_norm, a_k_norm, a_lq1, a_lk1, a_lq2,
              a_lk2, a_subln, kv_norm, w_kv, b_norm, b_w_in, b_w_out):
    pos = jnp.arange(x.shape[1])
    shared_k = None
    shared_v = None
    for layer in range(DEPTH):
        if layer < N_A_LAYERS:
            x = _diff_layer(x, pos, a_norm[layer], a_w_in[layer], a_w_out[layer],
                            a_q_norm[layer], a_k_norm[layer], a_lq1[layer], a_lk1[layer],
                            a_lq2[layer], a_lk2[layer], a_subln[layer], LAMBDA_INIT[layer])
        else:
            if layer == N_A_LAYERS:
                shared_k, shared_v = _shared_kv(x, kv_norm, w_kv)
            j = layer - N_A_LAYERS
            x = _sb_layer(x, shared_k, shared_v, b_norm[j], b_w_in[j], b_w_out[j])
    return x
```

```python
import functools
import math

import jax
import jax.numpy as jnp
from jax import lax
from jax.experimental import pallas as pl
from jax.experimental.pallas import tpu as pltpu

D_MODEL = 1024
HEAD_DIM = 64
LANES = 128
N_COL_BLOCKS = D_MODEL // LANES
N_A_LAYERS = 2
N_B_LAYERS = 2
LAMBDA_INIT = tuple(0.8 - 0.6 * math.exp(-0.3 * l) for l in range(N_A_LAYERS))
ROPE_THETA = 500000.0
ROT_DIM = HEAD_DIM // 4
RMS_EPS = 1e-6
QK_SCALE = HEAD_DIM ** -0.5

PROJ_ROWS = 512
PROJ_COLS = 256
ATT_BLOCK = 256
VMEM_LIMIT = 56 * 1024 * 1024

F32 = jnp.float32
BF16 = jnp.bfloat16


def _in_proj_kernel(*refs, kinds):
    n_qk = sum(k == "qk" for k in kinds)
    x_ref, gain_ref, w_ref = refs[:3]
    pos = 3
    if n_qk:
        bd_ref, cos_ref, sa_ref, sb_ref = refs[pos:pos + 4]
        pos += 4
        head_gain_refs = refs[pos:pos + n_qk]
        pos += n_qk
    out_refs = refs[pos:]

    x = x_ref[...]
    ms = jnp.mean(x * x, axis=-1, keepdims=True)
    xn = (x * lax.rsqrt(ms + RMS_EPS) * gain_ref[...]).astype(BF16)

    qk_seen = 0
    for gi, kind in enumerate(kinds):
        out_ref = out_refs[gi]
        if kind == "qk":
            head_gain = head_gain_refs[qk_seen][...]
            qk_seen += 1
        for c in range(0, D_MODEL, PROJ_COLS):
            col = gi * D_MODEL + c
            t = jnp.dot(xn, w_ref[:, col:col + PROJ_COLS], preferred_element_type=F32)
            if kind == "qk":
                for h in range(PROJ_COLS // LANES):
                    th = t[:, h * LANES:(h + 1) * LANES]
                    ss = jnp.dot((th * th).astype(BF16), bd_ref[...],
                                 preferred_element_type=F32)
                    tn = th * lax.rsqrt(ss * (1.0 / HEAD_DIM) + RMS_EPS) * head_gain
                    rot = (tn * cos_ref[...]
                           + pltpu.roll(tn, LANES - ROT_DIM // 2, 1) * sa_ref[...]
                           + pltpu.roll(tn, ROT_DIM // 2, 1) * sb_ref[...])
                    lo = c + h * LANES
                    out_ref[:, lo:lo + LANES] = rot.astype(BF16)
            elif kind == "bf16":
                out_ref[:, c:c + PROJ_COLS] = t.astype(BF16)
            elif kind == "bf16_scaled":
                out_ref[:, c:c + PROJ_COLS] = (t * QK_SCALE).astype(BF16)
            else:
                out_ref[:, c:c + PROJ_COLS] = t


def _in_proj(x2d, gain, w_bf16, kinds, seq_len, rope=None, head_gains=()):
    n_rows = x2d.shape[0]
    n_out = len(kinds)
    assert w_bf16.shape == (D_MODEL, n_out * D_MODEL)
    row_spec = pl.BlockSpec((PROJ_ROWS, D_MODEL), lambda i: (i, 0))
    const2 = lambda i: (0, 0)
    in_specs = [row_spec,
                pl.BlockSpec((1, D_MODEL), const2),
                pl.BlockSpec((D_MODEL, n_out * D_MODEL), const2)]
    args = [x2d, gain.reshape(1, D_MODEL), w_bf16]
    if rope is not None:
        blocks_per_seq = seq_len // PROJ_ROWS
        tab_spec = pl.BlockSpec((PROJ_ROWS, LANES), lambda i: (i % blocks_per_seq, 0))
        in_specs += [pl.BlockSpec((LANES, LANES), const2), tab_spec, tab_spec, tab_spec]
        args += list(rope)
        for hg in head_gains:
            in_specs.append(pl.BlockSpec((1, LANES), const2))
            args.append(hg)
    out_dtypes = [F32 if k == "f32" else BF16 for k in kinds]
    return pl.pallas_call(
        functools.partial(_in_proj_kernel, kinds=tuple(kinds)),
        grid=(n_rows // PROJ_ROWS,),
        in_specs=in_specs,
        out_specs=[row_spec] * n_out,
        out_shape=[jax.ShapeDtypeStruct((n_rows, D_MODEL), dt) for dt in out_dtypes],
        compiler_params=pltpu.CompilerParams(
            dimension_semantics=("arbitrary",), vmem_limit_bytes=VMEM_LIMIT),
        name="in_proj_" + "_".join(kinds),
    )(*args)


def _out_proj_kernel(og_ref, w_ref, x_ref, o_ref):
    o_ref[...] = x_ref[...] + jnp.dot(og_ref[...], w_ref[...], preferred_element_type=F32)


def _out_proj(og2d, w_bf16, x2d):
    n_rows = x2d.shape[0]
    row_spec = pl.BlockSpec((PROJ_ROWS, D_MODEL), lambda i: (i, 0))
    return pl.pallas_call(
        _out_proj_kernel,
        grid=(n_rows // PROJ_ROWS,),
        in_specs=[row_spec, pl.BlockSpec((D_MODEL, D_MODEL), lambda i: (0, 0)), row_spec],
        out_specs=row_spec,
        out_shape=jax.ShapeDtypeStruct((n_rows, D_MODEL), F32),
        compiler_params=pltpu.CompilerParams(
            dimension_semantics=("arbitrary",), vmem_limit_bytes=VMEM_LIMIT),
        name="out_proj",
    )(og2d, w_bf16, x2d)


def _stack_halves(q):
    lane = lax.broadcasted_iota(jnp.int32, q.shape, 1)
    zero = jnp.zeros_like(q)
    return jnp.concatenate(
        [jnp.where(lane < HEAD_DIM, q, zero), jnp.where(lane >= HEAD_DIM, q, zero)], axis=0)


def _scores(q2, kb):
    return lax.dot_general(q2, kb, (((1,), (1,)), ((), ())), preferred_element_type=F32)


def _silu(g):
    return g * (1.0 / (1.0 + jnp.exp(-g)))


def _diag_positions(t):
    r = lax.broadcasted_iota(jnp.int32, (2 * t, t), 0) & (t - 1)
    c = lax.broadcasted_iota(jnp.int32, (2 * t, t), 1)
    return r, c


def _diff_attn_kernel(q_ref, k_ref, v_ref, g_ref, sub_ref, lq1_ref, lk1_ref, lq2_ref,
                      lk2_ref, o_ref, m_scr, l_scr, acc_scr, *, lambda_init):
    t = ATT_BLOCK
    i = pl.program_id(2)
    q2 = _stack_halves(q_ref[0])
    m_scr[...] = jnp.full_like(m_scr, -jnp.inf)
    l_scr[...] = jnp.zeros_like(l_scr)
    acc_scr[...] = jnp.zeros_like(acc_scr)

    def step(j, masked):
        start = pl.multiple_of(j * t, t)
        kb = k_ref[0, pl.ds(start, t), :]
        vb = v_ref[0, pl.ds(start, t), :]
        s = _scores(q2, kb)
        if masked:
            r, c = _diag_positions(t)
            s = jnp.where(c <= r, s, -jnp.inf)
        m_prev = m_scr[...]
        m_new = jnp.maximum(m_prev, jnp.max(s, axis=-1, keepdims=True))
        alpha = jnp.exp(m_prev - m_new)
        p = jnp.exp(s - m_new)
        l_scr[...] = alpha * l_scr[...] + jnp.sum(p, axis=-1, keepdims=True)
        acc_scr[...] = alpha * acc_scr[...] + jnp.dot(
            p.astype(BF16), vb, preferred_element_type=F32)
        m_scr[...] = m_new

    def body(j, carry):
        step(j, False)
        return carry

    lax.fori_loop(0, i, body, 0)
    step(i, True)

    lam = (jnp.exp(jnp.sum(lq1_ref[...] * lk1_ref[...], axis=-1, keepdims=True))
           - jnp.exp(jnp.sum(lq2_ref[...] * lk2_ref[...], axis=-1, keepdims=True))
           + lambda_init)
    acc = acc_scr[...]
    l = l_scr[...]
    o = acc[:t] / l[:t] - lam * (acc[t:] / l[t:])
    ms = jnp.mean(o * o, axis=-1, keepdims=True)
    on = o * lax.rsqrt(ms + RMS_EPS) * sub_ref[...] * (1.0 - lambda_init)
    o_ref[0] = (on * _silu(g_ref[0])).astype(BF16)


def _diff_attention(q, k, v, g, subln, lq1, lk1, lq2, lk2, lambda_init):
    b, s, _ = q.shape
    t = ATT_BLOCK
    q_spec = pl.BlockSpec((1, t, LANES), lambda bi, h, i: (bi, i, h))
    kv_spec = pl.BlockSpec((1, s, LANES), lambda bi, h, i: (bi, 0, h))
    vec = lambda n: pl.BlockSpec((1, n), lambda bi, h, i: (0, 0))
    return pl.pallas_call(
        functools.partial(_diff_attn_kernel, lambda_init=lambda_init),
        grid=(b, N_COL_BLOCKS, s // t),
        in_specs=[q_spec, kv_spec, kv_spec, q_spec, vec(LANES),
                  vec(HEAD_DIM), vec(HEAD_DIM), vec(HEAD_DIM), vec(HEAD_DIM)],
        out_specs=q_spec,
        out_shape=jax.ShapeDtypeStruct((b, s, D_MODEL), BF16),
        scratch_shapes=[pltpu.VMEM((2 * t, 1), F32), pltpu.VMEM((2 * t, 1), F32),
                        pltpu.VMEM((2 * t, LANES), F32)],
        compiler_params=pltpu.CompilerParams(
            dimension_semantics=("arbitrary", "arbitrary", "arbitrary"),
            vmem_limit_bytes=VMEM_LIMIT),
        name="diff_attention",
    )(q, k, v, g, subln.reshape(1, LANES), lq1.reshape(1, HEAD_DIM), lk1.reshape(1, HEAD_DIM),
      lq2.reshape(1, HEAD_DIM), lk2.reshape(1, HEAD_DIM))


def _sb_attn_kernel(q_ref, k_ref, v_ref, g_ref, tri_ref, o_ref, carry_scr, acc_scr):
    t = ATT_BLOCK
    i = pl.program_id(2)
    q2 = _stack_halves(q_ref[0])
    carry_scr[...] = jnp.zeros_like(carry_scr)
    acc_scr[...] = jnp.zeros_like(acc_scr)

    def step(j, masked):
        start = pl.multiple_of(j * t, t)
        kb = k_ref[0, pl.ds(start, t), :]
        vb = v_ref[0, pl.ds(start, t), :]
        z = _scores(q2, kb)
        sp = jnp.maximum(z, 0.0) + jnp.log(1.0 + jnp.exp(-jnp.abs(z)))
        log_beta = z - sp
        log_1mb = -sp
        if masked:
            r, c = _diag_positions(t)
            keep = c < r
            log_1mb = jnp.where(keep, log_1mb, 0.0)
        hi = log_1mb.astype(BF16)
        lo = (log_1mb - hi.astype(F32)).astype(BF16)
        tri = tri_ref[...]
        inner = (jnp.dot(hi, tri, preferred_element_type=F32)
                 + jnp.dot(lo, tri, preferred_element_type=F32))
        carry = carry_scr[...]
        a = jnp.exp(log_beta + inner + carry)
        if masked:
            a = jnp.where(keep, a, 0.0)
        acc_scr[...] += jnp.dot(a.astype(BF16), vb, preferred_element_type=F32)
        carry_scr[...] = carry + inner[:, 0:1] + log_1mb[:, 0:1]

    step(i, True)

    def body(jj, c):
        step(i - 1 - jj, False)
        return c

    lax.fori_loop(0, i, body, 0)

    acc = acc_scr[...]
    lane = lax.broadcasted_iota(jnp.int32, (t, LANES), 1)
    o = jnp.where(lane < HEAD_DIM, acc[:t], acc[t:])
    o_ref[0] = (o * _silu(g_ref[0])).astype(BF16)


def _sb_attention(q, k, v, g, tri):
    b, s, _ = q.shape
    t = ATT_BLOCK
    q_spec = pl.BlockSpec((1, t, LANES), lambda bi, h, i: (bi, i, h))
    kv_spec = pl.BlockSpec((1, s, LANES), lambda bi, h, i: (bi, 0, h))
    return pl.pallas_call(
        _sb_attn_kernel,
        grid=(b, N_COL_BLOCKS, s // t),
        in_specs=[q_spec, kv_spec, kv_spec, q_spec,
                  pl.BlockSpec((t, t), lambda bi, h, i: (0, 0))],
        out_specs=q_spec,
        out_shape=jax.ShapeDtypeStruct((b, s, D_MODEL), BF16),
        scratch_shapes=[pltpu.VMEM((2 * t, 1), F32), pltpu.VMEM((2 * t, LANES), F32)],
        compiler_params=pltpu.CompilerParams(
            dimension_semantics=("arbitrary", "arbitrary", "arbitrary"),
            vmem_limit_bytes=VMEM_LIMIT),
        name="sb_attention",
    )(q, k, v, g, tri)


def _rope_tables(seq_len):
    half = ROT_DIM // 2
    inv = jnp.power(jnp.float32(ROPE_THETA), -jnp.arange(half, dtype=F32) / half)
    ang = jnp.arange(seq_len).astype(F32)[:, None] * inv[None, :]
    cos, sin = jnp.cos(ang), jnp.sin(ang)
    pad = jnp.zeros((seq_len, HEAD_DIM - ROT_DIM), F32)
    zeros_h = jnp.zeros((seq_len, half), F32)
    cos64 = jnp.concatenate([cos, cos, pad + 1.0], axis=-1)
    sa64 = jnp.concatenate([-sin, zeros_h, pad], axis=-1)
    sb64 = jnp.concatenate([zeros_h, sin, pad], axis=-1)
    rep = lambda a: jnp.concatenate([a, a], axis=-1)
    return rep(cos64), rep(sa64), rep(sb64)


def kernel(x, a_norm, a_w_in, a_w_out, a_q_norm, a_k_norm, a_lq1, a_lk1, a_lq2, a_lk2,
           a_subln, kv_norm, w_kv, b_norm, b_w_in, b_w_out):
    b, s, d = x.shape
    assert d == D_MODEL and s % PROJ_ROWS == 0 and s % ATT_BLOCK == 0
    n = b * s
    x2 = x.reshape(n, d)

    sub = jnp.arange(LANES) // HEAD_DIM
    block_diag = (sub[:, None] == sub[None, :]).astype(BF16)
    idx = jnp.arange(ATT_BLOCK)
    tri = (idx[:, None] > idx[None, :]).astype(BF16)
    rope = (block_diag,) + _rope_tables(s)
    tile2 = lambda v: jnp.concatenate([v, v]).reshape(1, LANES)

    for layer in range(N_A_LAYERS):
        q, k, v, g = _in_proj(
            x2, a_norm[layer], a_w_in[layer].astype(BF16), ("qk", "qk", "bf16", "f32"), s,
            rope=rope,
            head_gains=(tile2(a_q_norm[layer]) * QK_SCALE, tile2(a_k_norm[layer])))
        shp = (b, s, d)
        og = _diff_attention(q.reshape(shp), k.reshape(shp), v.reshape(shp), g.reshape(shp),
                             a_subln[layer], a_lq1[layer], a_lk1[layer], a_lq2[layer],
                             a_lk2[layer], LAMBDA_INIT[layer])
        x2 = _out_proj(og.reshape(n, d), a_w_out[layer].astype(BF16), x2)

    sk, sv = _in_proj(x2, kv_norm, w_kv.astype(BF16), ("bf16", "bf16"), s)
    for j in range(N_B_LAYERS):
        q, g = _in_proj(x2, b_norm[j], b_w_in[j].astype(BF16), ("bf16_scaled", "f32"), s)
        shp = (b, s, d)
        og = _sb_attention(q.reshape(shp), sk.reshape(shp), sv.reshape(shp), g.reshape(shp), tri)
        x2 = _out_proj(og.reshape(n, d), b_w_out[j].astype(BF16), x2)
    return x2.reshape(b, s, d)
```

```python
import functools
import math

import jax
import jax.numpy as jnp
from jax import lax
from jax.experimental import pallas as pl
from jax.experimental.pallas import tpu as pltpu

D_MODEL = 1024
HEAD_DIM = 64
LANES = 128
N_COL_BLOCKS = D_MODEL // LANES
N_A_LAYERS = 2
N_B_LAYERS = 2
LAMBDA_INIT = tuple(0.8 - 0.6 * math.exp(-0.3 * l) for l in range(N_A_LAYERS))
ROPE_THETA = 500000.0
ROT_DIM = HEAD_DIM // 4
RMS_EPS = 1e-6
LOG2E = 1.4426950408889634
SOFTPLUS_CLAMP = 64.0
Q_SCALE = HEAD_DIM ** -0.5 * LOG2E

PROJ_ROWS = 512
PROJ_COLS = 256
ATT_BLOCK = 256
ATT_HEADS = 8
SB_PAIRS = 4
VMEM_LIMIT = 56 * 1024 * 1024

F32 = jnp.float32
BF16 = jnp.bfloat16


def _in_proj_kernel(*refs, kinds):
    n_qk = sum(k == "qk" for k in kinds)
    x_ref, gain_ref, w_ref = refs[:3]
    pos = 3
    if n_qk:
        bd_ref, cos_ref, sa_ref, sb_ref = refs[pos:pos + 4]
        pos += 4
        head_gain_refs = refs[pos:pos + n_qk]
        pos += n_qk
    out_refs = refs[pos:]

    x = x_ref[...]
    ms = jnp.mean(x * x, axis=-1, keepdims=True)
    xn = (x * lax.rsqrt(ms + RMS_EPS) * gain_ref[...]).astype(BF16)

    qk_seen = 0
    for gi, kind in enumerate(kinds):
        out_ref = out_refs[gi]
        if kind == "qk":
            head_gain = head_gain_refs[qk_seen][...]
            qk_seen += 1
        for c in range(0, D_MODEL, PROJ_COLS):
            col = gi * D_MODEL + c
            t = jnp.dot(xn, w_ref[:, col:col + PROJ_COLS], preferred_element_type=F32)
            if kind == "qk":
                for h in range(PROJ_COLS // LANES):
                    th = t[:, h * LANES:(h + 1) * LANES]
                    ss = jnp.dot((th * th).astype(BF16), bd_ref[...],
                                 preferred_element_type=F32)
                    tn = th * lax.rsqrt(ss * (1.0 / HEAD_DIM) + RMS_EPS) * head_gain
                    rot = (tn * cos_ref[...]
                           + pltpu.roll(tn, LANES - ROT_DIM // 2, 1) * sa_ref[...]
                           + pltpu.roll(tn, ROT_DIM // 2, 1) * sb_ref[...])
                    lo = c + h * LANES
                    out_ref[:, lo:lo + LANES] = rot.astype(BF16)
            elif kind == "bf16":
                out_ref[:, c:c + PROJ_COLS] = t.astype(BF16)
            elif kind == "q_scaled":
                out_ref[:, c:c + PROJ_COLS] = (t * Q_SCALE).astype(BF16)
            else:
                out_ref[:, c:c + PROJ_COLS] = t


def _in_proj(x2d, gain, w_bf16, kinds, seq_len, rope=None, head_gains=()):
    n_rows = x2d.shape[0]
    n_out = len(kinds)
    assert w_bf16.shape == (D_MODEL, n_out * D_MODEL)
    row_spec = pl.BlockSpec((PROJ_ROWS, D_MODEL), lambda i: (i, 0))
    const2 = lambda i: (0, 0)
    in_specs = [row_spec,
                pl.BlockSpec((1, D_MODEL), const2),
                pl.BlockSpec((D_MODEL, n_out * D_MODEL), const2)]
    args = [x2d, gain.reshape(1, D_MODEL), w_bf16]
    if rope is not None:
        blocks_per_seq = seq_len // PROJ_ROWS
        tab_spec = pl.BlockSpec((PROJ_ROWS, LANES), lambda i: (i % blocks_per_seq, 0))
        in_specs += [pl.BlockSpec((LANES, LANES), const2), tab_spec, tab_spec, tab_spec]
        args += list(rope)
        for hg in head_gains:
            in_specs.append(pl.BlockSpec((1, LANES), const2))
            args.append(hg)
    out_dtypes = [F32 if k == "f32" else BF16 for k in kinds]
    return pl.pallas_call(
        functools.partial(_in_proj_kernel, kinds=tuple(kinds)),
        grid=(n_rows // PROJ_ROWS,),
        in_specs=in_specs,
        out_specs=[row_spec] * n_out,
        out_shape=[jax.ShapeDtypeStruct((n_rows, D_MODEL), dt) for dt in out_dtypes],
        compiler_params=pltpu.CompilerParams(
            dimension_semantics=("arbitrary",), vmem_limit_bytes=VMEM_LIMIT),
        name="in_proj_" + "_".join(kinds),
    )(*args)


def _out_proj_kernel(og_ref, w_ref, x_ref, o_ref):
    o_ref[...] = x_ref[...] + jnp.dot(og_ref[...], w_ref[...], preferred_element_type=F32)


def _out_proj(og2d, w_bf16, x2d):
    n_rows = x2d.shape[0]
    row_spec = pl.BlockSpec((PROJ_ROWS, D_MODEL), lambda i: (i, 0))
    return pl.pallas_call(
        _out_proj_kernel,
        grid=(n_rows // PROJ_ROWS,),
        in_specs=[row_spec, pl.BlockSpec((D_MODEL, D_MODEL), lambda i: (0, 0)), row_spec],
        out_specs=row_spec,
        out_shape=jax.ShapeDtypeStruct((n_rows, D_MODEL), F32),
        compiler_params=pltpu.CompilerParams(
            dimension_semantics=("arbitrary",), vmem_limit_bytes=VMEM_LIMIT),
        name="out_proj",
    )(og2d, w_bf16, x2d)


_TN = (((0,), (0,)), ((), ()))


def _stack_halves_t(q):
    qt = q.astype(F32).T.astype(BF16)
    row = lax.broadcasted_iota(jnp.int32, qt.shape, 0)
    zero = jnp.zeros_like(qt)
    return jnp.concatenate(
        [jnp.where(row < HEAD_DIM, qt, zero), jnp.where(row >= HEAD_DIM, qt, zero)], axis=1)


def _silu(g):
    return g * (1.0 / (1.0 + jnp.exp(-g)))


def _diag_positions(t):
    c = lax.broadcasted_iota(jnp.int32, (t, 2 * t), 0)
    r = lax.broadcasted_iota(jnp.int32, (t, 2 * t), 1) & (t - 1)
    return c, r


def _diff_attn_kernel(q_ref, k_ref, v_ref, g_ref, sub_ref, lq1_ref, lk1_ref, lq2_ref,
                      lk2_ref, o_ref, acc_scr, *, lambda_init):
    t = ATT_BLOCK
    i = pl.program_id(2)
    heads = range(ATT_HEADS)
    cols = [slice(h * LANES, (h + 1) * LANES) for h in heads]
    q2t = [_stack_halves_t(q_ref[0, :, cols[h]]) for h in heads]
    acc_scr[...] = jnp.zeros_like(acc_scr)

    def step(j, state, masked):
        start = pl.multiple_of(j * t, t)
        scores = [jnp.dot(k_ref[0, pl.ds(start, t), cols[h]], q2t[h],
                          preferred_element_type=F32) for h in heads]
        new_state, alphas, probs = [], [], []
        for h in heads:
            m_prev, l_prev = state[h]
            s = scores[h]
            if masked:
                c, r = _diag_positions(t)
                s = jnp.where(c <= r, s, -jnp.inf)
            m_new = jnp.maximum(m_prev, jnp.max(s, axis=0, keepdims=True))
            alpha = jnp.exp2(m_prev - m_new)
            p = jnp.exp2(s - m_new)
            new_state.append((m_new, alpha * l_prev + jnp.sum(p, axis=0, keepdims=True)))
            alphas.append(alpha)
            probs.append(p.astype(BF16))
        pvs = [lax.dot_general(v_ref[0, pl.ds(start, t), cols[h]], probs[h], _TN,
                               preferred_element_type=F32) for h in heads]
        for h in heads:
            acc_scr[h] = alphas[h] * acc_scr[h] + pvs[h]
        return tuple(new_state)

    m0 = jnp.full((1, 2 * t), -jnp.inf, F32)
    l0 = jnp.zeros((1, 2 * t), F32)
    state = lax.fori_loop(0, i, lambda j, st: step(j, st, False), ((m0, l0),) * ATT_HEADS)
    state = step(i, state, True)

    lam = (jnp.exp(jnp.sum(lq1_ref[...] * lk1_ref[...], axis=-1, keepdims=True))
           - jnp.exp(jnp.sum(lq2_ref[...] * lk2_ref[...], axis=-1, keepdims=True))
           + lambda_init)
    for h in heads:
        acc = acc_scr[h]
        l = state[h][1]
        o = acc[:, :t] / l[:, :t] - lam * (acc[:, t:] / l[:, t:])
        ms = jnp.mean(o * o, axis=0, keepdims=True)
        on = (o * lax.rsqrt(ms + RMS_EPS)).T
        o_ref[0, :, cols[h]] = (on * sub_ref[...] * (1.0 - lambda_init)
                                * _silu(g_ref[0, :, cols[h]])).astype(BF16)


def _diff_attention(q, k, v, g, subln, lq1, lk1, lq2, lk2, lambda_init):
    b, s, _ = q.shape
    t = ATT_BLOCK
    w = ATT_HEADS * LANES
    q_spec = pl.BlockSpec((1, t, w), lambda bi, h, i: (bi, i, h))
    kv_spec = pl.BlockSpec((1, s, w), lambda bi, h, i: (bi, 0, h),
                           pipeline_mode=pl.Buffered(1))
    vec = lambda n: pl.BlockSpec((1, n), lambda bi, h, i: (0, 0))
    return pl.pallas_call(
        functools.partial(_diff_attn_kernel, lambda_init=lambda_init),
        grid=(b, N_COL_BLOCKS // ATT_HEADS, s // t),
        in_specs=[q_spec, kv_spec, kv_spec, q_spec, vec(LANES),
                  vec(HEAD_DIM), vec(HEAD_DIM), vec(HEAD_DIM), vec(HEAD_DIM)],
        out_specs=q_spec,
        out_shape=jax.ShapeDtypeStruct((b, s, D_MODEL), BF16),
        scratch_shapes=[pltpu.VMEM((ATT_HEADS, LANES, 2 * t), F32)],
        compiler_params=pltpu.CompilerParams(
            dimension_semantics=("arbitrary", "arbitrary", "arbitrary"),
            vmem_limit_bytes=VMEM_LIMIT),
        name="diff_attention",
    )(q, k, v, g, subln.reshape(1, LANES), lq1.reshape(1, HEAD_DIM), lk1.reshape(1, HEAD_DIM),
      lq2.reshape(1, HEAD_DIM), lk2.reshape(1, HEAD_DIM))


def _sb_attn_kernel(q_ref, k_ref, v_ref, g_ref, tri_ref, o_ref, acc_scr):
    t = ATT_BLOCK
    i = pl.program_id(2)
    pairs = range(SB_PAIRS)
    cols = [slice(h * LANES, (h + 1) * LANES) for h in pairs]
    q2t = [_stack_halves_t(q_ref[0, :, cols[h]]) for h in pairs]
    acc_scr[...] = jnp.zeros_like(acc_scr)

    def step(j, carry, masked):
        start = pl.multiple_of(j * t, t)
        tri2 = tri_ref[...]
        zs = [jnp.dot(k_ref[0, pl.ds(start, t), cols[h]], q2t[h],
                      preferred_element_type=F32) for h in pairs]
        if masked:
            c, r = _diag_positions(t)
            keep = c < r
        splits = []
        for h in pairs:
            z = zs[h]
            sp = jnp.maximum(z, jnp.log2(1.0 + jnp.exp2(jnp.minimum(z, SOFTPLUS_CLAMP))))
            if masked:
                sp = jnp.where(keep, sp, 0.0)
            hi = sp.astype(BF16)
            lo = (sp - hi.astype(F32)).astype(BF16)
            splits.append(jnp.concatenate([hi, lo], axis=0))
        incls = [jnp.dot(tri2, splits[h], preferred_element_type=F32) for h in pairs]
        probs = []
        for h in pairs:
            a = jnp.exp2(zs[h] - incls[h] - carry[h])
            if masked:
                a = jnp.where(keep, a, 0.0)
            probs.append(a.astype(BF16))
        pvs = [lax.dot_general(v_ref[0, pl.ds(start, t), cols[h]], probs[h], _TN,
                               preferred_element_type=F32) for h in pairs]
        for h in pairs:
            acc_scr[h] += pvs[h]
        return tuple(carry[h] + incls[h][0:1, :] for h in pairs)

    zero = jnp.zeros((1, 2 * t), F32)
    carry = step(i, (zero,) * SB_PAIRS, True)
    lax.fori_loop(0, i, lambda jj, c: step(i - 1 - jj, c, False), carry)

    row = lax.broadcasted_iota(jnp.int32, (LANES, t), 0)
    for h in pairs:
        acc = acc_scr[h]
        o = jnp.where(row < HEAD_DIM, acc[:, :t], acc[:, t:]).T
        o_ref[0, :, cols[h]] = (o * _silu(g_ref[0, :, cols[h]])).astype(BF16)


def _sb_attention(q, k, v, g, tri):
    b, s, _ = q.shape
    t = ATT_BLOCK
    w = SB_PAIRS * LANES
    q_spec = pl.BlockSpec((1, t, w), lambda bi, h, i: (bi, i, h))
    kv_spec = pl.BlockSpec((1, s, w), lambda bi, h, i: (bi, 0, h),
                           pipeline_mode=pl.Buffered(1))
    return pl.pallas_call(
        _sb_attn_kernel,
        grid=(b, N_COL_BLOCKS // SB_PAIRS, s // t),
        in_specs=[q_spec, kv_spec, kv_spec, q_spec,
                  pl.BlockSpec((t, 2 * t), lambda bi, h, i: (0, 0))],
        out_specs=q_spec,
        out_shape=jax.ShapeDtypeStruct((b, s, D_MODEL), BF16),
        scratch_shapes=[pltpu.VMEM((SB_PAIRS, LANES, 2 * t), F32)],
        compiler_params=pltpu.CompilerParams(
            dimension_semantics=("arbitrary", "arbitrary", "arbitrary"),
            vmem_limit_bytes=VMEM_LIMIT),
        name="sb_attention",
    )(q, k, v, g, tri)


def _rope_tables(seq_len):
    half = ROT_DIM // 2
    inv = jnp.power(jnp.float32(ROPE_THETA), -jnp.arange(half, dtype=F32) / half)
    ang = jnp.arange(seq_len).astype(F32)[:, None] * inv[None, :]
    cos, sin = jnp.cos(ang), jnp.sin(ang)
    pad = jnp.zeros((seq_len, HEAD_DIM - ROT_DIM), F32)
    zeros_h = jnp.zeros((seq_len, half), F32)
    cos64 = jnp.concatenate([cos, cos, pad + 1.0], axis=-1)
    sa64 = jnp.concatenate([-sin, zeros_h, pad], axis=-1)
    sb64 = jnp.concatenate([zeros_h, sin, pad], axis=-1)
    rep = lambda a: jnp.concatenate([a, a], axis=-1)
    return rep(cos64), rep(sa64), rep(sb64)


def kernel(x, a_norm, a_w_in, a_w_out, a_q_norm, a_k_norm, a_lq1, a_lk1, a_lq2, a_lk2,
           a_subln, kv_norm, w_kv, b_norm, b_w_in, b_w_out):
    b, s, d = x.shape
    assert d == D_MODEL and s % PROJ_ROWS == 0 and s % ATT_BLOCK == 0
    n = b * s
    x2 = x.reshape(n, d)

    sub = jnp.arange(LANES) // HEAD_DIM
    block_diag = (sub[:, None] == sub[None, :]).astype(BF16)
    idx = jnp.arange(ATT_BLOCK)
    tri = (idx[None, :] >= idx[:, None]).astype(BF16)
    tri = jnp.concatenate([tri, tri], axis=1)
    rope = (block_diag,) + _rope_tables(s)
    tile2 = lambda v: jnp.concatenate([v, v]).reshape(1, LANES)

    for layer in range(N_A_LAYERS):
        q, k, v, g = _in_proj(
            x2, a_norm[layer], a_w_in[layer].astype(BF16), ("qk", "qk", "bf16", "f32"), s,
            rope=rope,
            head_gains=(tile2(a_q_norm[layer]) * Q_SCALE, tile2(a_k_norm[layer])))
        shp = (b, s, d)
        og = _diff_attention(q.reshape(shp), k.reshape(shp), v.reshape(shp), g.reshape(shp),
                             a_subln[layer], a_lq1[layer], a_lk1[layer], a_lq2[layer],
                             a_lk2[layer], LAMBDA_INIT[layer])
        x2 = _out_proj(og.reshape(n, d), a_w_out[layer].astype(BF16), x2)

    sk, sv = _in_proj(x2, kv_norm, w_kv.astype(BF16), ("bf16", "bf16"), s)
    for j in range(N_B_LAYERS):
        q, g = _in_proj(x2, b_norm[j], b_w_in[j].astype(BF16), ("q_scaled", "f32"), s)
        shp = (b, s, d)
        og = _sb_attention(q.reshape(shp), sk.reshape(shp), sv.reshape(shp), g.reshape(shp), tri)
        x2 = _out_proj(og.reshape(n, d), b_w_out[j].astype(BF16), x2)
    return x2.reshape(b, s, d)
```

```python
import functools
import math

import jax
import jax.numpy as jnp
from jax import lax
from jax.experimental import pallas as pl
from jax.experimental.pallas import tpu as pltpu

D_MODEL = 1024
HEAD_DIM = 64
LANES = 128
N_COL_BLOCKS = D_MODEL // LANES
N_A_LAYERS = 2
N_B_LAYERS = 2
LAMBDA_INIT = tuple(0.8 - 0.6 * math.exp(-0.3 * l) for l in range(N_A_LAYERS))
ROPE_THETA = 500000.0
ROT_DIM = HEAD_DIM // 4
RMS_EPS = 1e-6
LOG2E = 1.4426950408889634
SOFTPLUS_CLAMP = 64.0
SB_DEAD_CARRY = 160.0
Q_SCALE = HEAD_DIM ** -0.5 * LOG2E

PROJ_ROWS = 512
PROJ_COLS = 256
ATT_BLOCK = 256
ATT_HEADS = 8
SB_PAIRS = 4
VMEM_LIMIT = 56 * 1024 * 1024

F32 = jnp.float32
BF16 = jnp.bfloat16


def _in_proj_kernel(*refs, kinds):
    n_qk = sum(k == "qk" for k in kinds)
    x_ref, gain_ref, w_ref = refs[:3]
    pos = 3
    if n_qk:
        bd_ref, cos_ref, sa_ref, sb_ref = refs[pos:pos + 4]
        pos += 4
        head_gain_refs = refs[pos:pos + n_qk]
        pos += n_qk
    out_refs = refs[pos:]

    x = x_ref[...]
    ms = jnp.mean(x * x, axis=-1, keepdims=True)
    xn = (x * lax.rsqrt(ms + RMS_EPS) * gain_ref[...]).astype(BF16)

    qk_seen = 0
    for gi, kind in enumerate(kinds):
        out_ref = out_refs[gi]
        if kind == "qk":
            head_gain = head_gain_refs[qk_seen][...]
            qk_seen += 1
        for c in range(0, D_MODEL, PROJ_COLS):
            col = gi * D_MODEL + c
            t = jnp.dot(xn, w_ref[:, col:col + PROJ_COLS], preferred_element_type=F32)
            if kind == "qk":
                for h in range(PROJ_COLS // LANES):
                    th = t[:, h * LANES:(h + 1) * LANES]
                    ss = jnp.dot((th * th).astype(BF16), bd_ref[...],
                                 preferred_element_type=F32)
                    tn = th * lax.rsqrt(ss * (1.0 / HEAD_DIM) + RMS_EPS) * head_gain
                    rot = (tn * cos_ref[...]
                           + pltpu.roll(tn, LANES - ROT_DIM // 2, 1) * sa_ref[...]
                           + pltpu.roll(tn, ROT_DIM // 2, 1) * sb_ref[...])
                    lo = c + h * LANES
                    out_ref[:, lo:lo + LANES] = rot.astype(BF16)
            elif kind == "bf16":
                out_ref[:, c:c + PROJ_COLS] = t.astype(BF16)
            elif kind == "q_scaled":
                out_ref[:, c:c + PROJ_COLS] = (t * Q_SCALE).astype(BF16)
            else:
                out_ref[:, c:c + PROJ_COLS] = t


def _in_proj(x2d, gain, w_bf16, kinds, seq_len, rope=None, head_gains=()):
    n_rows = x2d.shape[0]
    n_out = len(kinds)
    assert w_bf16.shape == (D_MODEL, n_out * D_MODEL)
    row_spec = pl.BlockSpec((PROJ_ROWS, D_MODEL), lambda i: (i, 0))
    const2 = lambda i: (0, 0)
    in_specs = [row_spec,
                pl.BlockSpec((1, D_MODEL), const2),
                pl.BlockSpec((D_MODEL, n_out * D_MODEL), const2)]
    args = [x2d, gain.reshape(1, D_MODEL), w_bf16]
    if rope is not None:
        blocks_per_seq = seq_len // PROJ_ROWS
        tab_spec = pl.BlockSpec((PROJ_ROWS, LANES), lambda i: (i % blocks_per_seq, 0))
        in_specs += [pl.BlockSpec((LANES, LANES), const2), tab_spec, tab_spec, tab_spec]
        args += list(rope)
        for hg in head_gains:
            in_specs.append(pl.BlockSpec((1, LANES), const2))
            args.append(hg)
    out_dtypes = [F32 if k == "f32" else BF16 for k in kinds]
    return pl.pallas_call(
        functools.partial(_in_proj_kernel, kinds=tuple(kinds)),
        grid=(n_rows // PROJ_ROWS,),
        in_specs=in_specs,
        out_specs=[row_spec] * n_out,
        out_shape=[jax.ShapeDtypeStruct((n_rows, D_MODEL), dt) for dt in out_dtypes],
        compiler_params=pltpu.CompilerParams(
            dimension_semantics=("arbitrary",), vmem_limit_bytes=VMEM_LIMIT),
        name="in_proj_" + "_".join(kinds),
    )(*args)


def _out_proj_kernel(og_ref, w_ref, x_ref, o_ref):
    o_ref[...] = x_ref[...] + jnp.dot(og_ref[...], w_ref[...], preferred_element_type=F32)


def _out_proj(og2d, w_bf16, x2d):
    n_rows = x2d.shape[0]
    row_spec = pl.BlockSpec((PROJ_ROWS, D_MODEL), lambda i: (i, 0))
    return pl.pallas_call(
        _out_proj_kernel,
        grid=(n_rows // PROJ_ROWS,),
        in_specs=[row_spec, pl.BlockSpec((D_MODEL, D_MODEL), lambda i: (0, 0)), row_spec],
        out_specs=row_spec,
        out_shape=jax.ShapeDtypeStruct((n_rows, D_MODEL), F32),
        compiler_params=pltpu.CompilerParams(
            dimension_semantics=("arbitrary",), vmem_limit_bytes=VMEM_LIMIT),
        name="out_proj",
    )(og2d, w_bf16, x2d)


_TN = (((0,), (0,)), ((), ()))


def _stack_halves_t(q):
    qt = q.astype(F32).T.astype(BF16)
    row = lax.broadcasted_iota(jnp.int32, qt.shape, 0)
    zero = jnp.zeros_like(qt)
    return jnp.concatenate(
        [jnp.where(row < HEAD_DIM, qt, zero), jnp.where(row >= HEAD_DIM, qt, zero)], axis=1)


def _silu(g):
    return g * (1.0 / (1.0 + jnp.exp(-g)))


def _diag_positions(t):
    c = lax.broadcasted_iota(jnp.int32, (t, 2 * t), 0)
    r = lax.broadcasted_iota(jnp.int32, (t, 2 * t), 1) & (t - 1)
    return c, r


def _diff_attn_kernel(q_ref, k_ref, v_ref, g_ref, sub_ref, lq1_ref, lk1_ref, lq2_ref,
                      lk2_ref, o_ref, acc_scr, *, lambda_init):
    t = ATT_BLOCK
    i = pl.program_id(2)
    heads = range(ATT_HEADS)
    cols = [slice(h * LANES, (h + 1) * LANES) for h in heads]
    q2t = [_stack_halves_t(q_ref[0, :, cols[h]]) for h in heads]
    acc_scr[...] = jnp.zeros_like(acc_scr)

    def step(j, state, masked):
        start = pl.multiple_of(j * t, t)
        scores = [jnp.dot(k_ref[0, pl.ds(start, t), cols[h]], q2t[h],
                          preferred_element_type=F32) for h in heads]
        new_state, alphas, probs = [], [], []
        for h in heads:
            m_prev, l_prev = state[h]
            s = scores[h]
            if masked:
                c, r = _diag_positions(t)
                s = jnp.where(c <= r, s, -jnp.inf)
            m_new = jnp.maximum(m_prev, jnp.max(s, axis=0, keepdims=True))
            alpha = jnp.exp2(m_prev - m_new)
            p = jnp.exp2(s - m_new)
            new_state.append((m_new, alpha * l_prev + jnp.sum(p, axis=0, keepdims=True)))
            alphas.append(alpha)
            probs.append(p.astype(BF16))
        pvs = [lax.dot_general(v_ref[0, pl.ds(start, t), cols[h]], probs[h], _TN,
                               preferred_element_type=F32) for h in heads]
        for h in heads:
            acc_scr[h] = alphas[h] * acc_scr[h] + pvs[h]
        return tuple(new_state)

    m0 = jnp.full((1, 2 * t), -jnp.inf, F32)
    l0 = jnp.zeros((1, 2 * t), F32)
    state = lax.fori_loop(0, i, lambda j, st: step(j, st, False), ((m0, l0),) * ATT_HEADS)
    state = step(i, state, True)

    lam = (jnp.exp(jnp.sum(lq1_ref[...] * lk1_ref[...], axis=-1, keepdims=True))
           - jnp.exp(jnp.sum(lq2_ref[...] * lk2_ref[...], axis=-1, keepdims=True))
           + lambda_init)
    for h in heads:
        acc = acc_scr[h]
        l = state[h][1]
        o = acc[:, :t] / l[:, :t] - lam * (acc[:, t:] / l[:, t:])
        ms = jnp.mean(o * o, axis=0, keepdims=True)
        on = (o * lax.rsqrt(ms + RMS_EPS)).T
        o_ref[0, :, cols[h]] = (on * sub_ref[...] * (1.0 - lambda_init)
                                * _silu(g_ref[0, :, cols[h]])).astype(BF16)


def _diff_attention(q, k, v, g, subln, lq1, lk1, lq2, lk2, lambda_init):
    b, s, _ = q.shape
    t = ATT_BLOCK
    w = ATT_HEADS * LANES
    q_spec = pl.BlockSpec((1, t, w), lambda bi, h, i: (bi, i, h))
    kv_spec = pl.BlockSpec((1, s, w), lambda bi, h, i: (bi, 0, h),
                           pipeline_mode=pl.Buffered(1))
    vec = lambda n: pl.BlockSpec((1, n), lambda bi, h, i: (0, 0))
    return pl.pallas_call(
        functools.partial(_diff_attn_kernel, lambda_init=lambda_init),
        grid=(b, N_COL_BLOCKS // ATT_HEADS, s // t),
        in_specs=[q_spec, kv_spec, kv_spec, q_spec, vec(LANES),
                  vec(HEAD_DIM), vec(HEAD_DIM), vec(HEAD_DIM), vec(HEAD_DIM)],
        out_specs=q_spec,
        out_shape=jax.ShapeDtypeStruct((b, s, D_MODEL), BF16),
        scratch_shapes=[pltpu.VMEM((ATT_HEADS, LANES, 2 * t), F32)],
        compiler_params=pltpu.CompilerParams(
            dimension_semantics=("arbitrary", "arbitrary", "arbitrary"),
            vmem_limit_bytes=VMEM_LIMIT),
        name="diff_attention",
    )(q, k, v, g, subln.reshape(1, LANES), lq1.reshape(1, HEAD_DIM), lk1.reshape(1, HEAD_DIM),
      lq2.reshape(1, HEAD_DIM), lk2.reshape(1, HEAD_DIM))


def _sb_attn_kernel(q_ref, k_ref, v_ref, g_ref, tri_ref, o_ref, acc_scr):
    t = ATT_BLOCK
    i = pl.program_id(2)
    pairs = range(SB_PAIRS)
    cols = [slice(h * LANES, (h + 1) * LANES) for h in pairs]
    q2t = [_stack_halves_t(q_ref[0, :, cols[h]]) for h in pairs]
    acc_scr[...] = jnp.zeros_like(acc_scr)

    def step(j, carry, masked):
        start = pl.multiple_of(j * t, t)
        tri2 = tri_ref[...]
        zs = [jnp.dot(k_ref[0, pl.ds(start, t), cols[h]], q2t[h],
                      preferred_element_type=F32) for h in pairs]
        if masked:
            c, r = _diag_positions(t)
            keep = c < r
        splits = []
        for h in pairs:
            z = zs[h]
            sp = jnp.maximum(z, jnp.log2(1.0 + jnp.exp2(jnp.minimum(z, SOFTPLUS_CLAMP))))
            if masked:
                sp = jnp.where(keep, sp, 0.0)
            hi = sp.astype(BF16)
            lo = (sp - hi.astype(F32)).astype(BF16)
            splits.append(jnp.concatenate([hi, lo], axis=0))
        incls = [jnp.dot(tri2, splits[h], preferred_element_type=F32) for h in pairs]
        probs = []
        for h in pairs:
            a = jnp.exp2(zs[h] - incls[h] - carry[h])
            if masked:
                a = jnp.where(keep, a, 0.0)
            probs.append(a.astype(BF16))
        pvs = [lax.dot_general(v_ref[0, pl.ds(start, t), cols[h]], probs[h], _TN,
                               preferred_element_type=F32) for h in pairs]
        for h in pairs:
            acc_scr[h] += pvs[h]
        return tuple(carry[h] + incls[h][0:1, :] for h in pairs)

    def smallest(carry):
        return jnp.min(functools.reduce(jnp.minimum, carry))

    zero = jnp.zeros((1, 2 * t), F32)
    carry = step(i, (zero,) * SB_PAIRS, True)

    def keep_walking(st):
        jj, _, low = st
        return jnp.logical_and(jj < i, low < SB_DEAD_CARRY)

    def walk(st):
        jj, c, _ = st
        c = step(i - 1 - jj, c, False)
        return jj + 1, c, smallest(c)

    lax.while_loop(keep_walking, walk, (jnp.int32(0), carry, smallest(carry)))

    row = lax.broadcasted_iota(jnp.int32, (LANES, t), 0)
    for h in pairs:
        acc = acc_scr[h]
        o = jnp.where(row < HEAD_DIM, acc[:, :t], acc[:, t:]).T
        o_ref[0, :, cols[h]] = (o * _silu(g_ref[0, :, cols[h]])).astype(BF16)


def _sb_attention(q, k, v, g, tri):
    b, s, _ = q.shape
    t = ATT_BLOCK
    w = SB_PAIRS * LANES
    q_spec = pl.BlockSpec((1, t, w), lambda bi, h, i: (bi, i, h))
    kv_spec = pl.BlockSpec((1, s, w), lambda bi, h, i: (bi, 0, h),
                           pipeline_mode=pl.Buffered(1))
    return pl.pallas_call(
        _sb_attn_kernel,
        grid=(b, N_COL_BLOCKS // SB_PAIRS, s // t),
        in_specs=[q_spec, kv_spec, kv_spec, q_spec,
                  pl.BlockSpec((t, 2 * t), lambda bi, h, i: (0, 0))],
        out_specs=q_spec,
        out_shape=jax.ShapeDtypeStruct((b, s, D_MODEL), BF16),
        scratch_shapes=[pltpu.VMEM((SB_PAIRS, LANES, 2 * t), F32)],
        compiler_params=pltpu.CompilerParams(
            dimension_semantics=("arbitrary", "arbitrary", "arbitrary"),
            vmem_limit_bytes=VMEM_LIMIT),
        name="sb_attention",
    )(q, k, v, g, tri)


def _rope_tables(seq_len):
    half = ROT_DIM // 2
    inv = jnp.power(jnp.float32(ROPE_THETA), -jnp.arange(half, dtype=F32) / half)
    ang = jnp.arange(seq_len).astype(F32)[:, None] * inv[None, :]
    cos, sin = jnp.cos(ang), jnp.sin(ang)
    pad = jnp.zeros((seq_len, HEAD_DIM - ROT_DIM), F32)
    zeros_h = jnp.zeros((seq_len, half), F32)
    cos64 = jnp.concatenate([cos, cos, pad + 1.0], axis=-1)
    sa64 = jnp.concatenate([-sin, zeros_h, pad], axis=-1)
    sb64 = jnp.concatenate([zeros_h, sin, pad], axis=-1)
    rep = lambda a: jnp.concatenate([a, a], axis=-1)
    return rep(cos64), rep(sa64), rep(sb64)


def kernel(x, a_norm, a_w_in, a_w_out, a_q_norm, a_k_norm, a_lq1, a_lk1, a_lq2, a_lk2,
           a_subln, kv_norm, w_kv, b_norm, b_w_in, b_w_out):
    b, s, d = x.shape
    assert d == D_MODEL and s % PROJ_ROWS == 0 and s % ATT_BLOCK == 0
    n = b * s
    x2 = x.reshape(n, d)

    sub = jnp.arange(LANES) // HEAD_DIM
    block_diag = (sub[:, None] == sub[None, :]).astype(BF16)
    idx = jnp.arange(ATT_BLOCK)
    tri = (idx[None, :] >= idx[:, None]).astype(BF16)
    tri = jnp.concatenate([tri, tri], axis=1)
    rope = (block_diag,) + _rope_tables(s)
    tile2 = lambda v: jnp.concatenate([v, v]).reshape(1, LANES)

    for layer in range(N_A_LAYERS):
        q, k, v, g = _in_proj(
            x2, a_norm[layer], a_w_in[layer].astype(BF16), ("qk", "qk", "bf16", "f32"), s,
            rope=rope,
            head_gains=(tile2(a_q_norm[layer]) * Q_SCALE, tile2(a_k_norm[layer])))
        shp = (b, s, d)
        og = _diff_attention(q.reshape(shp), k.reshape(shp), v.reshape(shp), g.reshape(shp),
                             a_subln[layer], a_lq1[layer], a_lk1[layer], a_lq2[layer],
                             a_lk2[layer], LAMBDA_INIT[layer])
        x2 = _out_proj(og.reshape(n, d), a_w_out[layer].astype(BF16), x2)

    sk, sv = _in_proj(x2, kv_norm, w_kv.astype(BF16), ("bf16", "bf16"), s)
    for j in range(N_B_LAYERS):
        q, g = _in_proj(x2, b_norm[j], b_w_in[j].astype(BF16), ("q_scaled", "f32"), s)
        shp = (b, s, d)
        og = _sb_attention(q.reshape(shp), sk.reshape(shp), sv.reshape(shp), g.reshape(shp), tri)
        x2 = _out_proj(og.reshape(n, d), b_w_out[j].astype(BF16), x2)
    return x2.reshape(b, s, d)
```

```python
import functools
import math

import jax
import jax.numpy as jnp
from jax import lax
from jax.experimental import pallas as pl
from jax.experimental.pallas import tpu as pltpu

D_MODEL = 1024
HEAD_DIM = 64
LANES = 128
N_COL_BLOCKS = D_MODEL // LANES
N_A_LAYERS = 2
N_B_LAYERS = 2
LAMBDA_INIT = tuple(0.8 - 0.6 * math.exp(-0.3 * l) for l in range(N_A_LAYERS))
ROPE_THETA = 500000.0
ROT_DIM = HEAD_DIM // 4
RMS_EPS = 1e-6
LOG2E = 1.4426950408889634
SOFTPLUS_CLAMP = 64.0
SB_DEAD_CARRY = 160.0
Q_SCALE = HEAD_DIM ** -0.5 * LOG2E

PROJ_ROWS = 512
PROJ_COLS = 256
ATT_BLOCK = 256
ATT_HEADS = 8
ONES_ROWS = 16
SB_PAIRS = 8
VMEM_LIMIT = 56 * 1024 * 1024

F32 = jnp.float32
BF16 = jnp.bfloat16


def _in_proj_kernel(*refs, kinds):
    n_qk = sum(k == "qk" for k in kinds)
    x_ref, gain_ref, w_ref = refs[:3]
    pos = 3
    if n_qk:
        bd_ref, cos_ref, sa_ref, sb_ref = refs[pos:pos + 4]
        pos += 4
        head_gain_refs = refs[pos:pos + n_qk]
        pos += n_qk
    out_refs = refs[pos:]

    x = x_ref[...]
    ms = jnp.mean(x * x, axis=-1, keepdims=True)
    xn = (x * lax.rsqrt(ms + RMS_EPS) * gain_ref[...]).astype(BF16)

    head_gains = {}
    for gi, kind in enumerate(kinds):
        if kind == "qk":
            head_gains[gi] = head_gain_refs[len(head_gains)][...]

    def finish(t, gi, c):
        kind, out_ref = kinds[gi], out_refs[gi]
        if kind == "qk":
            ss = jnp.dot((t * t).astype(BF16), bd_ref[...], preferred_element_type=F32)
            tn = t * lax.rsqrt(ss * (1.0 / HEAD_DIM) + RMS_EPS)
            for h in range(PROJ_COLS // LANES):
                th = tn[:, h * LANES:(h + 1) * LANES] * head_gains[gi]
                rot = (th * cos_ref[...]
                       + pltpu.roll(th, LANES - ROT_DIM // 2, 1) * sa_ref[...]
                       + pltpu.roll(th, ROT_DIM // 2, 1) * sb_ref[...])
                lo = c + h * LANES
                out_ref[:, lo:lo + LANES] = rot.astype(BF16)
        elif kind in ("vt", "vt1"):
            for h in range(PROJ_COLS // LANES):
                blk = (c + h * LANES) // LANES
                out_ref[0, blk, 0:LANES, :] = t[:, h * LANES:(h + 1) * LANES].T.astype(BF16)
                if kind == "vt1":
                    out_ref[0, blk, LANES:LANES + ONES_ROWS, :] = jnp.ones(
                        (ONES_ROWS, t.shape[0]), BF16)
        elif kind == "bf16":
            out_ref[:, c:c + PROJ_COLS] = t.astype(BF16)
        elif kind == "q_scaled":
            out_ref[:, c:c + PROJ_COLS] = (t * Q_SCALE).astype(BF16)
        else:
            out_ref[:, c:c + PROJ_COLS] = t

    def project(gi, c):
        col = gi * D_MODEL + c
        return jnp.dot(xn, w_ref[:, col:col + PROJ_COLS], preferred_element_type=F32)

    chunks = [(gi, c) for gi in range(len(kinds)) for c in range(0, D_MODEL, PROJ_COLS)]
    t_next = project(*chunks[0])
    for n, (gi, c) in enumerate(chunks):
        t = t_next
        if n + 1 < len(chunks):
            t_next = project(*chunks[n + 1])
        finish(t, gi, c)


def _in_proj(x2d, gain, w_bf16, kinds, seq_len, rope=None, head_gains=()):
    n_rows = x2d.shape[0]
    n_out = len(kinds)
    assert w_bf16.shape == (D_MODEL, n_out * D_MODEL)
    blocks_per_seq = seq_len // PROJ_ROWS
    row_spec = pl.BlockSpec((PROJ_ROWS, D_MODEL), lambda i: (i, 0))
    const2 = lambda i: (0, 0)
    in_specs = [row_spec,
                pl.BlockSpec((1, D_MODEL), const2),
                pl.BlockSpec((D_MODEL, n_out * D_MODEL), const2)]
    args = [x2d, gain.reshape(1, D_MODEL), w_bf16]
    if rope is not None:
        tab_spec = pl.BlockSpec((PROJ_ROWS, LANES), lambda i: (i % blocks_per_seq, 0))
        in_specs += [pl.BlockSpec((PROJ_COLS, PROJ_COLS), const2), tab_spec, tab_spec, tab_spec]
        args += list(rope)
        for hg in head_gains:
            in_specs.append(pl.BlockSpec((1, LANES), const2))
            args.append(hg)
    out_specs, out_shape = [], []
    for k in kinds:
        if k in ("vt", "vt1"):
            rows = LANES + (ONES_ROWS if k == "vt1" else 0)
            out_specs.append(pl.BlockSpec(
                (1, N_COL_BLOCKS, rows, PROJ_ROWS),
                lambda i: (i // blocks_per_seq, 0, 0, i % blocks_per_seq)))
            out_shape.append(jax.ShapeDtypeStruct(
                (n_rows // seq_len, N_COL_BLOCKS, rows, seq_len), BF16))
        else:
            out_specs.append(row_spec)
            out_shape.append(jax.ShapeDtypeStruct((n_rows, D_MODEL), F32 if k == "f32" else BF16))
    return pl.pallas_call(
        functools.partial(_in_proj_kernel, kinds=tuple(kinds)),
        grid=(n_rows // PROJ_ROWS,),
        in_specs=in_specs,
        out_specs=out_specs,
        out_shape=out_shape,
        compiler_params=pltpu.CompilerParams(
            dimension_semantics=("arbitrary",), vmem_limit_bytes=VMEM_LIMIT),
        name="in_proj_" + "_".join(kinds),
    )(*args)


def _out_proj_kernel(og_ref, w_ref, x_ref, o_ref):
    o_ref[...] = x_ref[...] + jnp.dot(og_ref[...], w_ref[...], preferred_element_type=F32)


def _out_proj(og2d, w_bf16, x2d):
    n_rows = x2d.shape[0]
    row_spec = pl.BlockSpec((PROJ_ROWS, D_MODEL), lambda i: (i, 0))
    return pl.pallas_call(
        _out_proj_kernel,
        grid=(n_rows // PROJ_ROWS,),
        in_specs=[row_spec, pl.BlockSpec((D_MODEL, D_MODEL), lambda i: (0, 0)), row_spec],
        out_specs=row_spec,
        out_shape=jax.ShapeDtypeStruct((n_rows, D_MODEL), F32),
        compiler_params=pltpu.CompilerParams(
            dimension_semantics=("arbitrary",), vmem_limit_bytes=VMEM_LIMIT),
        name="out_proj",
    )(og2d, w_bf16, x2d)


def _stack_halves_t(q):
    qt = q.astype(F32).T.astype(BF16)
    row = lax.broadcasted_iota(jnp.int32, qt.shape, 0)
    zero = jnp.zeros_like(qt)
    return jnp.concatenate(
        [jnp.where(row < HEAD_DIM, qt, zero), jnp.where(row >= HEAD_DIM, qt, zero)], axis=1)


def _silu(g):
    return g * (1.0 / (1.0 + jnp.exp(-g)))


def _diag_positions(t):
    c = lax.broadcasted_iota(jnp.int32, (t, 2 * t), 0)
    r = lax.broadcasted_iota(jnp.int32, (t, 2 * t), 1) & (t - 1)
    return c, r


def _diff_attn_kernel(q_ref, k_ref, vt_ref, g_ref, sub_ref, lq1_ref, lk1_ref, lq2_ref,
                      lk2_ref, o_ref, acc_scr, *, lambda_init):
    t = ATT_BLOCK
    i = pl.program_id(2)
    heads = range(ATT_HEADS)
    cols = [slice(h * LANES, (h + 1) * LANES) for h in heads]
    q2t = [_stack_halves_t(q_ref[0, :, cols[h]]) for h in heads]
    acc_scr[...] = jnp.zeros_like(acc_scr)

    def step(j, state, masked):
        start = pl.multiple_of(j * t, t)
        scores = [jnp.dot(k_ref[0, pl.ds(start, t), cols[h]], q2t[h],
                          preferred_element_type=F32) for h in heads]
        maxes, alphas, probs = [], [], []
        for h in heads:
            m_prev = state[h][0]
            s = scores[h]
            if masked:
                c, r = _diag_positions(t)
                s = jnp.where(c <= r, s, -jnp.inf)
            m_new = jnp.maximum(m_prev, jnp.max(s, axis=0, keepdims=True))
            maxes.append(m_new)
            alphas.append(jnp.exp2(m_prev - m_new))
            probs.append(jnp.exp2(s - m_new).astype(BF16))
        pvs = [jnp.dot(vt_ref[0, h, :, pl.ds(start, t)], probs[h],
                       preferred_element_type=F32) for h in heads]
        new_state = []
        for h in heads:
            acc_scr[h] = alphas[h] * acc_scr[h] + pvs[h][:LANES]
            new_state.append((maxes[h], alphas[h] * state[h][1] + pvs[h][LANES:LANES + 1]))
        return tuple(new_state)

    m0 = jnp.full((1, 2 * t), -jnp.inf, F32)
    l0 = jnp.zeros((1, 2 * t), F32)
    state = lax.fori_loop(0, i, lambda j, st: step(j, st, False), ((m0, l0),) * ATT_HEADS)
    state = step(i, state, True)

    lam = (jnp.exp(jnp.sum(lq1_ref[...] * lk1_ref[...], axis=-1, keepdims=True))
           - jnp.exp(jnp.sum(lq2_ref[...] * lk2_ref[...], axis=-1, keepdims=True))
           + lambda_init)
    for h in heads:
        acc = acc_scr[h]
        l = state[h][1]
        o = acc[:, :t] / l[:, :t] - lam * (acc[:, t:] / l[:, t:])
        ms = jnp.mean(o * o, axis=0, keepdims=True)
        on = (o * lax.rsqrt(ms + RMS_EPS)).T
        o_ref[0, :, cols[h]] = (on * sub_ref[...] * (1.0 - lambda_init)
                                * _silu(g_ref[0, :, cols[h]])).astype(BF16)


def _diff_attention(q, k, vt, g, subln, lq1, lk1, lq2, lk2, lambda_init):
    b, s, _ = q.shape
    t = ATT_BLOCK
    w = ATT_HEADS * LANES
    q_spec = pl.BlockSpec((1, t, w), lambda bi, h, i: (bi, i, h))
    k_spec = pl.BlockSpec((1, s, w), lambda bi, h, i: (bi, 0, h),
                          pipeline_mode=pl.Buffered(1))
    vt_spec = pl.BlockSpec((1, ATT_HEADS, LANES + ONES_ROWS, s), lambda bi, h, i: (bi, h, 0, 0),
                           pipeline_mode=pl.Buffered(1))
    vec = lambda n: pl.BlockSpec((1, n), lambda bi, h, i: (0, 0))
    return pl.pallas_call(
        functools.partial(_diff_attn_kernel, lambda_init=lambda_init),
        grid=(b, N_COL_BLOCKS // ATT_HEADS, s // t),
        in_specs=[q_spec, k_spec, vt_spec, q_spec, vec(LANES),
                  vec(HEAD_DIM), vec(HEAD_DIM), vec(HEAD_DIM), vec(HEAD_DIM)],
        out_specs=q_spec,
        out_shape=jax.ShapeDtypeStruct((b, s, D_MODEL), BF16),
        scratch_shapes=[pltpu.VMEM((ATT_HEADS, LANES, 2 * t), F32)],
        compiler_params=pltpu.CompilerParams(
            dimension_semantics=("arbitrary", "arbitrary", "arbitrary"),
            vmem_limit_bytes=VMEM_LIMIT),
        name="diff_attention",
    )(q, k, vt, g, subln.reshape(1, LANES), lq1.reshape(1, HEAD_DIM), lk1.reshape(1, HEAD_DIM),
      lq2.reshape(1, HEAD_DIM), lk2.reshape(1, HEAD_DIM))


def _sb_attn_kernel(q_ref, k_ref, vt_ref, g_ref, tri_ref, o_ref, acc_scr):
    t = ATT_BLOCK
    i = pl.program_id(2)
    pairs = range(SB_PAIRS)
    cols = [slice(h * LANES, (h + 1) * LANES) for h in pairs]
    q2t = [_stack_halves_t(q_ref[0, :, cols[h]]) for h in pairs]
    acc_scr[...] = jnp.zeros_like(acc_scr)

    def step(j, carry, masked):
        start = pl.multiple_of(j * t, t)
        tri2 = tri_ref[...]
        zs = [jnp.dot(k_ref[0, pl.ds(start, t), cols[h]], q2t[h],
                      preferred_element_type=F32) for h in pairs]
        if masked:
            c, r = _diag_positions(t)
            keep = c < r
        splits = []
        for h in pairs:
            z = zs[h]
            sp = jnp.maximum(z, jnp.log2(1.0 + jnp.exp2(jnp.minimum(z, SOFTPLUS_CLAMP))))
            if masked:
                sp = jnp.where(keep, sp, 0.0)
            hi = sp.astype(BF16)
            lo = (sp - hi.astype(F32)).astype(BF16)
            splits.append(jnp.concatenate([hi, lo], axis=0))
        incls = [jnp.dot(tri2, splits[h], preferred_element_type=F32) for h in pairs]
        probs = []
        for h in pairs:
            a = jnp.exp2(zs[h] - incls[h] - carry[h])
            if masked:
                a = jnp.where(keep, a, 0.0)
            probs.append(a.astype(BF16))
        pvs = []
        for h in pairs:
            vt = vt_ref[0, h, :, pl.ds(start, t)]
            pvs.append(jnp.concatenate(
                [jnp.dot(vt[:HEAD_DIM], probs[h][:, :t], preferred_element_type=F32),
                 jnp.dot(vt[HEAD_DIM:], probs[h][:, t:], preferred_element_type=F32)], axis=0))
        for h in pairs:
            acc_scr[h] += pvs[h]
        return tuple(carry[h] + incls[h][0:1, :] for h in pairs)

    def smallest(carry):
        return jnp.min(functools.reduce(jnp.minimum, carry))

    zero = jnp.zeros((1, 2 * t), F32)
    carry = step(i, (zero,) * SB_PAIRS, True)

    def keep_walking(st):
        jj, _, low = st
        return jnp.logical_and(jj < i, low < SB_DEAD_CARRY)

    def walk(st):
        jj, c, _ = st
        c = step(i - 1 - jj, c, False)
        return jj + 1, c, smallest(c)

    lax.while_loop(keep_walking, walk, (jnp.int32(0), carry, smallest(carry)))

    for h in pairs:
        o = acc_scr[h].T
        o_ref[0, :, cols[h]] = (o * _silu(g_ref[0, :, cols[h]])).astype(BF16)


def _sb_attention(q, k, vt, g, tri):
    b, s, _ = q.shape
    t = ATT_BLOCK
    w = SB_PAIRS * LANES
    q_spec = pl.BlockSpec((1, t, w), lambda bi, h, i: (bi, i, h))
    k_spec = pl.BlockSpec((1, s, w), lambda bi, h, i: (bi, 0, h),
                          pipeline_mode=pl.Buffered(1))
    vt_spec = pl.BlockSpec((1, SB_PAIRS, LANES, s), lambda bi, h, i: (bi, h, 0, 0),
                           pipeline_mode=pl.Buffered(1))
    return pl.pallas_call(
        _sb_attn_kernel,
        grid=(b, N_COL_BLOCKS // SB_PAIRS, s // t),
        in_specs=[q_spec, k_spec, vt_spec, q_spec,
                  pl.BlockSpec((t, 2 * t), lambda bi, h, i: (0, 0))],
        out_specs=q_spec,
        out_shape=jax.ShapeDtypeStruct((b, s, D_MODEL), BF16),
        scratch_shapes=[pltpu.VMEM((SB_PAIRS, LANES, t), F32)],
        compiler_params=pltpu.CompilerParams(
            dimension_semantics=("arbitrary", "arbitrary", "arbitrary"),
            vmem_limit_bytes=VMEM_LIMIT),
        name="sb_attention",
    )(q, k, vt, g, tri)


def _rope_tables(seq_len):
    half = ROT_DIM // 2
    inv = jnp.power(jnp.float32(ROPE_THETA), -jnp.arange(half, dtype=F32) / half)
    ang = jnp.arange(seq_len).astype(F32)[:, None] * inv[None, :]
    cos, sin = jnp.cos(ang), jnp.sin(ang)
    pad = jnp.zeros((seq_len, HEAD_DIM - ROT_DIM), F32)
    zeros_h = jnp.zeros((seq_len, half), F32)
    cos64 = jnp.concatenate([cos, cos, pad + 1.0], axis=-1)
    sa64 = jnp.concatenate([-sin, zeros_h, pad], axis=-1)
    sb64 = jnp.concatenate([zeros_h, sin, pad], axis=-1)
    rep = lambda a: jnp.concatenate([a, a], axis=-1)
    return rep(cos64), rep(sa64), rep(sb64)


def kernel(x, a_norm, a_w_in, a_w_out, a_q_norm, a_k_norm, a_lq1, a_lk1, a_lq2, a_lk2,
           a_subln, kv_norm, w_kv, b_norm, b_w_in, b_w_out):
    b, s, d = x.shape
    assert d == D_MODEL and s % PROJ_ROWS == 0 and s % ATT_BLOCK == 0
    n = b * s
    x2 = x.reshape(n, d)

    sub = jnp.arange(PROJ_COLS) // HEAD_DIM
    block_diag = (sub[:, None] == sub[None, :]).astype(BF16)
    idx = jnp.arange(ATT_BLOCK)
    tri = (idx[None, :] >= idx[:, None]).astype(BF16)
    tri = jnp.concatenate([tri, tri], axis=1)
    rope = (block_diag,) + _rope_tables(s)
    tile2 = lambda v: jnp.concatenate([v, v]).reshape(1, LANES)

    for layer in range(N_A_LAYERS):
        q, k, vt, g = _in_proj(
            x2, a_norm[layer], a_w_in[layer].astype(BF16), ("qk", "qk", "vt1", "f32"), s,
            rope=rope,
            head_gains=(tile2(a_q_norm[layer]) * Q_SCALE, tile2(a_k_norm[layer])))
        shp = (b, s, d)
        og = _diff_attention(q.reshape(shp), k.reshape(shp), vt, g.reshape(shp),
                             a_subln[layer], a_lq1[layer], a_lk1[layer], a_lq2[layer],
                             a_lk2[layer], LAMBDA_INIT[layer])
        x2 = _out_proj(og.reshape(n, d), a_w_out[layer].astype(BF16), x2)

    sk, svt = _in_proj(x2, kv_norm, w_kv.astype(BF16), ("bf16", "vt"), s)
    for j in range(N_B_LAYERS):
        q, g = _in_proj(x2, b_norm[j], b_w_in[j].astype(BF16), ("q_scaled", "f32"), s)
        shp = (b, s, d)
        og = _sb_attention(q.reshape(shp), sk.reshape(shp), svt, g.reshape(shp), tri)
        x2 = _out_proj(og.reshape(n, d), b_w_out[j].astype(BF16), x2)
    return x2.reshape(b, s, d)
```

```python
import functools
import math

import jax
import jax.numpy as jnp
from jax import lax
from jax.experimental import pallas as pl
from jax.experimental.pallas import tpu as pltpu

D_MODEL = 1024
HEAD_DIM = 64
LANES = 128
N_COL_BLOCKS = D_MODEL // LANES
N_A_LAYERS = 2
N_B_LAYERS = 2
LAMBDA_INIT = tuple(0.8 - 0.6 * math.exp(-0.3 * l) for l in range(N_A_LAYERS))
ROPE_THETA = 500000.0
ROT_DIM = HEAD_DIM // 4
RMS_EPS = 1e-6
LOG2E = 1.4426950408889634
SOFTPLUS_CLAMP = 64.0
SB_DEAD_CARRY = 160.0
Q_SCALE = HEAD_DIM ** -0.5 * LOG2E

PROJ_ROWS = 512
PROJ_COLS = 256
KEY_BLOCK = 256
DIFF_Q_BLOCK = 256
ATT_HEADS = 8
ONES_ROWS = 16
SB_PAIRS = 8
VMEM_LIMIT = 56 * 1024 * 1024

F32 = jnp.float32
BF16 = jnp.bfloat16


def _in_proj_kernel(*refs, kinds):
    n_qk = sum(k == "qk" for k in kinds)
    x_ref, gain_ref, w_ref = refs[:3]
    pos = 3
    if n_qk:
        bd_ref, cos_ref, sa_ref, sb_ref = refs[pos:pos + 4]
        pos += 4
        head_gain_refs = refs[pos:pos + n_qk]
        pos += n_qk
    out_refs = refs[pos:]

    x = x_ref[...]
    ms = jnp.mean(x * x, axis=-1, keepdims=True)
    xn = (x * lax.rsqrt(ms + RMS_EPS) * gain_ref[...]).astype(BF16)

    head_gains = {}
    for gi, kind in enumerate(kinds):
        if kind == "qk":
            head_gains[gi] = head_gain_refs[len(head_gains)][...]

    def finish(t, gi, c):
        kind, out_ref = kinds[gi], out_refs[gi]
        if kind == "qk":
            ss = jnp.dot((t * t).astype(BF16), bd_ref[...], preferred_element_type=F32)
            tn = t * lax.rsqrt(ss * (1.0 / HEAD_DIM) + RMS_EPS)
            for h in range(PROJ_COLS // LANES):
                th = tn[:, h * LANES:(h + 1) * LANES] * head_gains[gi]
                rot = (th * cos_ref[...]
                       + pltpu.roll(th, LANES - ROT_DIM // 2, 1) * sa_ref[...]
                       + pltpu.roll(th, ROT_DIM // 2, 1) * sb_ref[...])
                lo = c + h * LANES
                out_ref[:, lo:lo + LANES] = rot.astype(BF16)
        elif kind in ("vt", "vt1"):
            for h in range(PROJ_COLS // LANES):
                blk = (c + h * LANES) // LANES
                out_ref[0, blk, 0:LANES, :] = t[:, h * LANES:(h + 1) * LANES].T.astype(BF16)
                if kind == "vt1":
                    out_ref[0, blk, LANES:LANES + ONES_ROWS, :] = jnp.ones(
                        (ONES_ROWS, t.shape[0]), BF16)
        elif kind == "bf16":
            out_ref[:, c:c + PROJ_COLS] = t.astype(BF16)
        elif kind == "q_scaled":
            out_ref[:, c:c + PROJ_COLS] = (t * Q_SCALE).astype(BF16)
        else:
            out_ref[:, c:c + PROJ_COLS] = t

    def project(gi, c):
        col = gi * D_MODEL + c
        return jnp.dot(xn, w_ref[:, col:col + PROJ_COLS], preferred_element_type=F32)

    chunks = [(gi, c) for gi in range(len(kinds)) for c in range(0, D_MODEL, PROJ_COLS)]
    t_next = project(*chunks[0])
    for n, (gi, c) in enumerate(chunks):
        t = t_next
        if n + 1 < len(chunks):
            t_next = project(*chunks[n + 1])
        finish(t, gi, c)


def _in_proj(x2d, gain, w_bf16, kinds, seq_len, rope=None, head_gains=()):
    n_rows = x2d.shape[0]
    n_out = len(kinds)
    assert w_bf16.shape == (D_MODEL, n_out * D_MODEL)
    blocks_per_seq = seq_len // PROJ_ROWS
    row_spec = pl.BlockSpec((PROJ_ROWS, D_MODEL), lambda i: (i, 0))
    const2 = lambda i: (0, 0)
    in_specs = [row_spec,
                pl.BlockSpec((1, D_MODEL), const2),
                pl.BlockSpec((D_MODEL, n_out * D_MODEL), const2)]
    args = [x2d, gain.reshape(1, D_MODEL), w_bf16]
    if rope is not None:
        tab_spec = pl.BlockSpec((PROJ_ROWS, LANES), lambda i: (i % blocks_per_seq, 0))
        in_specs += [pl.BlockSpec((PROJ_COLS, PROJ_COLS), const2), tab_spec, tab_spec, tab_spec]
        args += list(rope)
        for hg in head_gains:
            in_specs.append(pl.BlockSpec((1, LANES), const2))
            args.append(hg)
    out_specs, out_shape = [], []
    for k in kinds:
        if k in ("vt", "vt1"):
            rows = LANES + (ONES_ROWS if k == "vt1" else 0)
            out_specs.append(pl.BlockSpec(
                (1, N_COL_BLOCKS, rows, PROJ_ROWS),
                lambda i: (i // blocks_per_seq, 0, 0, i % blocks_per_seq)))
            out_shape.append(jax.ShapeDtypeStruct(
                (n_rows // seq_len, N_COL_BLOCKS, rows, seq_len), BF16))
        else:
            out_specs.append(row_spec)
            out_shape.append(jax.ShapeDtypeStruct((n_rows, D_MODEL), F32 if k == "f32" else BF16))
    return pl.pallas_call(
        functools.partial(_in_proj_kernel, kinds=tuple(kinds)),
        grid=(n_rows // PROJ_ROWS,),
        in_specs=in_specs,
        out_specs=out_specs,
        out_shape=out_shape,
        compiler_params=pltpu.CompilerParams(
            dimension_semantics=("arbitrary",), vmem_limit_bytes=VMEM_LIMIT),
        name="in_proj_" + "_".join(kinds),
    )(*args)


def _out_proj_kernel(og_ref, w_ref, x_ref, o_ref):
    o_ref[...] = x_ref[...] + jnp.dot(og_ref[...], w_ref[...], preferred_element_type=F32)


def _out_proj(og2d, w_bf16, x2d):
    n_rows = x2d.shape[0]
    row_spec = pl.BlockSpec((PROJ_ROWS, D_MODEL), lambda i: (i, 0))
    return pl.pallas_call(
        _out_proj_kernel,
        grid=(n_rows // PROJ_ROWS,),
        in_specs=[row_spec, pl.BlockSpec((D_MODEL, D_MODEL), lambda i: (0, 0)), row_spec],
        out_specs=row_spec,
        out_shape=jax.ShapeDtypeStruct((n_rows, D_MODEL), F32),
        compiler_params=pltpu.CompilerParams(
            dimension_semantics=("arbitrary",), vmem_limit_bytes=VMEM_LIMIT),
        name="out_proj",
    )(og2d, w_bf16, x2d)


def _stack_halves_t(q):
    qt = q.astype(F32).T.astype(BF16)
    row = lax.broadcasted_iota(jnp.int32, qt.shape, 0)
    zero = jnp.zeros_like(qt)
    return jnp.concatenate(
        [jnp.where(row < HEAD_DIM, qt, zero), jnp.where(row >= HEAD_DIM, qt, zero)], axis=1)


def _silu(g):
    return g * (1.0 / (1.0 + jnp.exp(-g)))


def _diag_positions(tk, tq, key_off=0):
    c = lax.broadcasted_iota(jnp.int32, (tk, 2 * tq), 0) + key_off
    r = lax.broadcasted_iota(jnp.int32, (tk, 2 * tq), 1) & (tq - 1)
    return c, r


def _diff_attn_kernel(q_ref, k_ref, vt_ref, g_ref, sub_ref, lq1_ref, lk1_ref, lq2_ref,
                      lk2_ref, o_ref, acc_scr, s_scr, *, lambda_init):
    tq, tk = DIFF_Q_BLOCK, KEY_BLOCK
    i = pl.program_id(2)
    heads = range(ATT_HEADS)
    cols = [slice(h * LANES, (h + 1) * LANES) for h in heads]
    q2t = [_stack_halves_t(q_ref[0, :, cols[h]]) for h in heads]
    acc_scr[...] = jnp.zeros_like(acc_scr)

    def scores(j, h):
        start = pl.multiple_of(j * tk, tk)
        return jnp.dot(k_ref[0, pl.ds(start, tk), cols[h]], q2t[h],
                       preferred_element_type=F32)

    def step(j, state, diag=None, last=False):
        start = pl.multiple_of(j * tk, tk)
        new_state = []
        for h in heads:
            m_prev, l_prev = state[h]
            s = s_scr[h]
            if diag is not None:
                c, r = _diag_positions(tk, tq, diag * tk)
                s = jnp.where(c <= r, s, -jnp.inf)
            m_new = jnp.maximum(m_prev, jnp.max(s, axis=0, keepdims=True))
            alpha = jnp.exp2(m_prev - m_new)
            p = jnp.exp2(s - m_new).astype(BF16)
            if not last:
                s_scr[h] = scores(j + 1, h)
            pv = jnp.dot(vt_ref[0, h, :, pl.ds(start, tk)], p,
                         preferred_element_type=F32)
            acc_scr[h] = alpha * acc_scr[h] + pv[:LANES]
            new_state.append((m_new, alpha * l_prev + pv[LANES:LANES + 1]))
        return tuple(new_state)

    for h in heads:
        s_scr[h] = scores(0, h)
    m0 = jnp.full((1, 2 * tq), -jnp.inf, F32)
    l0 = jnp.zeros((1, 2 * tq), F32)
    n_diag = tq // tk
    state = lax.fori_loop(0, i * n_diag, lambda j, st: step(j, st), ((m0, l0),) * ATT_HEADS)
    for d in range(n_diag):
        state = step(i * n_diag + d, state, diag=d, last=d == n_diag - 1)

    lam = (jnp.exp(jnp.sum(lq1_ref[...] * lk1_ref[...], axis=-1, keepdims=True))
           - jnp.exp(jnp.sum(lq2_ref[...] * lk2_ref[...], axis=-1, keepdims=True))
           + lambda_init)
    for h in heads:
        acc = acc_scr[h]
        l = state[h][1]
        o = acc[:, :tq] / l[:, :tq] - lam * (acc[:, tq:] / l[:, tq:])
        ms = jnp.mean(o * o, axis=0, keepdims=True)
        on = (o * lax.rsqrt(ms + RMS_EPS)).T
        o_ref[0, :, cols[h]] = (on * sub_ref[...] * (1.0 - lambda_init)
                                * _silu(g_ref[0, :, cols[h]])).astype(BF16)


def _diff_attention(q, k, vt, g, subln, lq1, lk1, lq2, lk2, lambda_init):
    b, s, _ = q.shape
    t = DIFF_Q_BLOCK
    w = ATT_HEADS * LANES
    q_spec = pl.BlockSpec((1, t, w), lambda bi, h, i: (bi, i, h))
    k_spec = pl.BlockSpec((1, s, w), lambda bi, h, i: (bi, 0, h),
                          pipeline_mode=pl.Buffered(1))
    vt_spec = pl.BlockSpec((1, ATT_HEADS, LANES + ONES_ROWS, s), lambda bi, h, i: (bi, h, 0, 0),
                           pipeline_mode=pl.Buffered(1))
    vec = lambda n: pl.BlockSpec((1, n), lambda bi, h, i: (0, 0))
    return pl.pallas_call(
        functools.partial(_diff_attn_kernel, lambda_init=lambda_init),
        grid=(b, N_COL_BLOCKS // ATT_HEADS, s // t),
        in_specs=[q_spec, k_spec, vt_spec, q_spec, vec(LANES),
                  vec(HEAD_DIM), vec(HEAD_DIM), vec(HEAD_DIM), vec(HEAD_DIM)],
        out_specs=q_spec,
        out_shape=jax.ShapeDtypeStruct((b, s, D_MODEL), BF16),
        scratch_shapes=[pltpu.VMEM((ATT_HEADS, LANES, 2 * t), F32),
                        pltpu.VMEM((ATT_HEADS, KEY_BLOCK, 2 * t), F32)],
        compiler_params=pltpu.CompilerParams(
            dimension_semantics=("arbitrary", "arbitrary", "arbitrary"),
            vmem_limit_bytes=VMEM_LIMIT),
        name="diff_attention",
    )(q, k, vt, g, subln.reshape(1, LANES), lq1.reshape(1, HEAD_DIM), lk1.reshape(1, HEAD_DIM),
      lq2.reshape(1, HEAD_DIM), lk2.reshape(1, HEAD_DIM))


def _sb_attn_kernel(q_ref, k_ref, vt_ref, g_ref, tri_ref, o_ref, acc_scr):
    t = KEY_BLOCK
    i = pl.program_id(2)
    pairs = range(SB_PAIRS)
    cols = [slice(h * LANES, (h + 1) * LANES) for h in pairs]
    q2t = [_stack_halves_t(q_ref[0, :, cols[h]]) for h in pairs]
    acc_scr[...] = jnp.zeros_like(acc_scr)

    def step(j, carry, masked):
        start = pl.multiple_of(j * t, t)
        tri = tri_ref[...]
        zs = [jnp.dot(k_ref[0, pl.ds(start, t), cols[h]], q2t[h],
                      preferred_element_type=F32) for h in pairs]
        if masked:
            c, r = _diag_positions(t, t)
            keep = c < r
        sps = []
        for h in pairs:
            z = zs[h]
            sp = jnp.maximum(z, jnp.log2(1.0 + jnp.exp2(jnp.minimum(z, SOFTPLUS_CLAMP))))
            if masked:
                sp = jnp.where(keep, sp, 0.0)
            sps.append(sp.astype(BF16))
        incls = [jnp.dot(tri, sps[h], preferred_element_type=F32) for h in pairs]
        probs = []
        for h in pairs:
            a = jnp.exp2(zs[h] - incls[h] - carry[h])
            if masked:
                a = jnp.where(keep, a, 0.0)
            probs.append(a.astype(BF16))
        pvs = []
        for h in pairs:
            vt = vt_ref[0, h, :, pl.ds(start, t)]
            pvs.append(jnp.concatenate(
                [jnp.dot(vt[:HEAD_DIM], probs[h][:, :t], preferred_element_type=F32),
                 jnp.dot(vt[HEAD_DIM:], probs[h][:, t:], preferred_element_type=F32)], axis=0))
        for h in pairs:
            acc_scr[h] += pvs[h]
        return tuple(carry[h] + incls[h][0:1, :] for h in pairs)

    def smallest(carry):
        return jnp.min(functools.reduce(jnp.minimum, carry))

    zero = jnp.zeros((1, 2 * t), F32)
    carry = step(i, (zero,) * SB_PAIRS, True)

    def keep_walking(st):
        jj, _, low = st
        return jnp.logical_and(jj < i, low < SB_DEAD_CARRY)

    def walk(st):
        jj, c, _ = st
        c = step(i - 1 - jj, c, False)
        return jj + 1, c, smallest(c)

    lax.while_loop(keep_walking, walk, (jnp.int32(0), carry, smallest(carry)))

    for h in pairs:
        o = acc_scr[h].T
        o_ref[0, :, cols[h]] = (o * _silu(g_ref[0, :, cols[h]])).astype(BF16)


def _sb_attention(q, k, vt, g, tri):
    b, s, _ = q.shape
    t = KEY_BLOCK
    w = SB_PAIRS * LANES
    q_spec = pl.BlockSpec((1, t, w), lambda bi, h, i: (bi, i, h))
    k_spec = pl.BlockSpec((1, s, w), lambda bi, h, i: (bi, 0, h),
                          pipeline_mode=pl.Buffered(1))
    vt_spec = pl.BlockSpec((1, SB_PAIRS, LANES, s), lambda bi, h, i: (bi, h, 0, 0),
                           pipeline_mode=pl.Buffered(1))
    return pl.pallas_call(
        _sb_attn_kernel,
        grid=(b, N_COL_BLOCKS // SB_PAIRS, s // t),
        in_specs=[q_spec, k_spec, vt_spec, q_spec,
                  pl.BlockSpec((t, t), lambda bi, h, i: (0, 0))],
        out_specs=q_spec,
        out_shape=jax.ShapeDtypeStruct((b, s, D_MODEL), BF16),
        scratch_shapes=[pltpu.VMEM((SB_PAIRS, LANES, t), F32)],
        compiler_params=pltpu.CompilerParams(
            dimension_semantics=("arbitrary", "arbitrary", "arbitrary"),
            vmem_limit_bytes=VMEM_LIMIT),
        name="sb_attention",
    )(q, k, vt, g, tri)


def _rope_tables(seq_len):
    half = ROT_DIM // 2
    inv = jnp.power(jnp.float32(ROPE_THETA), -jnp.arange(half, dtype=F32) / half)
    ang = jnp.arange(seq_len).astype(F32)[:, None] * inv[None, :]
    cos, sin = jnp.cos(ang), jnp.sin(ang)
    pad = jnp.zeros((seq_len, HEAD_DIM - ROT_DIM), F32)
    zeros_h = jnp.zeros((seq_len, half), F32)
    cos64 = jnp.concatenate([cos, cos, pad + 1.0], axis=-1)
    sa64 = jnp.concatenate([-sin, zeros_h, pad], axis=-1)
    sb64 = jnp.concatenate([zeros_h, sin, pad], axis=-1)
    rep = lambda a: jnp.concatenate([a, a], axis=-1)
    return rep(cos64), rep(sa64), rep(sb64)


def kernel(x, a_norm, a_w_in, a_w_out, a_q_norm, a_k_norm, a_lq1, a_lk1, a_lq2, a_lk2,
           a_subln, kv_norm, w_kv, b_norm, b_w_in, b_w_out):
    b, s, d = x.shape
    assert d == D_MODEL and s % PROJ_ROWS == 0 and s % DIFF_Q_BLOCK == 0
    assert DIFF_Q_BLOCK % KEY_BLOCK == 0
    n = b * s
    x2 = x.reshape(n, d)

    sub = jnp.arange(PROJ_COLS) // HEAD_DIM
    block_diag = (sub[:, None] == sub[None, :]).astype(BF16)
    idx = jnp.arange(KEY_BLOCK)
    tri = (idx[None, :] >= idx[:, None]).astype(BF16)
    rope = (block_diag,) + _rope_tables(s)
    tile2 = lambda v: jnp.concatenate([v, v]).reshape(1, LANES)

    for layer in range(N_A_LAYERS):
        q, k, vt, g = _in_proj(
            x2, a_norm[layer], a_w_in[layer].astype(BF16), ("qk", "qk", "vt1", "f32"), s,
            rope=rope,
            head_gains=(tile2(a_q_norm[layer]) * Q_SCALE, tile2(a_k_norm[layer])))
        shp = (b, s, d)
        og = _diff_attention(q.reshape(shp), k.reshape(shp), vt, g.reshape(shp),
                             a_subln[layer], a_lq1[layer], a_lk1[layer], a_lq2[layer],
                             a_lk2[layer], LAMBDA_INIT[layer])
        x2 = _out_proj(og.reshape(n, d), a_w_out[layer].astype(BF16), x2)

    sk, svt = _in_proj(x2, kv_norm, w_kv.astype(BF16), ("bf16", "vt"), s)
    for j in range(N_B_LAYERS):
        q, g = _in_proj(x2, b_norm[j], b_w_in[j].astype(BF16), ("q_scaled", "f32"), s)
        shp = (b, s, d)
        og = _sb_attention(q.reshape(shp), sk.reshape(shp), svt, g.reshape(shp), tri)
        x2 = _out_proj(og.reshape(n, d), b_w_out[j].astype(BF16), x2)
    return x2.reshape(b, s, d)
```

```python
import functools
import math

import jax
import jax.numpy as jnp
from jax import lax
from jax.experimental import pallas as pl
from jax.experimental.pallas import tpu as pltpu

D_MODEL = 1024
HEAD_DIM = 64
LANES = 128
N_COL_BLOCKS = D_MODEL // LANES
N_A_LAYERS = 2
N_B_LAYERS = 2
LAMBDA_INIT = tuple(0.8 - 0.6 * math.exp(-0.3 * l) for l in range(N_A_LAYERS))
ROPE_THETA = 500000.0
ROT_DIM = HEAD_DIM // 4
RMS_EPS = 1e-6
LOG2E = 1.4426950408889634
SOFTPLUS_CLAMP = 64.0
SB_DEAD_CARRY = 160.0
Q_SCALE = HEAD_DIM ** -0.5 * LOG2E

PROJ_ROWS = 512
PROJ_COLS = 256
KEY_BLOCK = 256
DIFF_Q_BLOCK = 256
ATT_HEADS = 8
ONES_ROWS = 16
SB_PAIRS = 8
VMEM_LIMIT = 56 * 1024 * 1024

F32 = jnp.float32
BF16 = jnp.bfloat16


def _in_proj_kernel(*refs, kinds, gain_ids, residual):
    n_qk = sum(k == "qk" for k in kinds)
    x_ref, gain_ref, w_ref = refs[:3]
    pos = 3
    if residual:
        og_ref, w_out_ref = refs[pos:pos + 2]
        pos += 2
    if n_qk:
        bd_ref, cos_ref, sa_ref, sb_ref = refs[pos:pos + 4]
        pos += 4
        head_gain_refs = refs[pos:pos + n_qk]
        pos += n_qk
    out_refs = refs[pos:]

    x = x_ref[...]
    if residual:
        x = x + jnp.dot(og_ref[...], w_out_ref[...], preferred_element_type=F32)
        out_refs[len(kinds)][...] = x
    ms = jnp.mean(x * x, axis=-1, keepdims=True)
    xr = x * lax.rsqrt(ms + RMS_EPS)
    xn = {g: (xr * gain_ref[g:g + 1, :]).astype(BF16) for g in sorted(set(gain_ids))}

    head_gains = {}
    for gi, kind in enumerate(kinds):
        if kind == "qk":
            head_gains[gi] = head_gain_refs[len(head_gains)][...]

    def finish(t, gi, c):
        kind, out_ref = kinds[gi], out_refs[gi]
        if kind == "qk":
            ss = jnp.dot((t * t).astype(BF16), bd_ref[...], preferred_element_type=F32)
            tn = t * lax.rsqrt(ss * (1.0 / HEAD_DIM) + RMS_EPS)
            for h in range(PROJ_COLS // LANES):
                th = tn[:, h * LANES:(h + 1) * LANES] * head_gains[gi]
                rot = (th * cos_ref[...]
                       + pltpu.roll(th, LANES - ROT_DIM // 2, 1) * sa_ref[...]
                       + pltpu.roll(th, ROT_DIM // 2, 1) * sb_ref[...])
                lo = c + h * LANES
                out_ref[:, lo:lo + LANES] = rot.astype(BF16)
        elif kind in ("vt", "vt1"):
            for h in range(PROJ_COLS // LANES):
                blk = (c + h * LANES) // LANES
                out_ref[0, blk, 0:LANES, :] = t[:, h * LANES:(h + 1) * LANES].T.astype(BF16)
                if kind == "vt1":
                    out_ref[0, blk, LANES:LANES + ONES_ROWS, :] = jnp.ones(
                        (ONES_ROWS, t.shape[0]), BF16)
        elif kind == "bf16":
            out_ref[:, c:c + PROJ_COLS] = t.astype(BF16)
        elif kind == "q_scaled":
            out_ref[:, c:c + PROJ_COLS] = (t * Q_SCALE).astype(BF16)
        else:
            out_ref[:, c:c + PROJ_COLS] = t

    def project(gi, c):
        col = gi * D_MODEL + c
        return jnp.dot(xn[gain_ids[gi]], w_ref[:, col:col + PROJ_COLS],
                       preferred_element_type=F32)

    chunks = [(gi, c) for gi in range(len(kinds)) for c in range(0, D_MODEL, PROJ_COLS)]
    t_next = project(*chunks[0])
    for n, (gi, c) in enumerate(chunks):
        t = t_next
        if n + 1 < len(chunks):
            t_next = project(*chunks[n + 1])
        finish(t, gi, c)


def _in_proj(x2d, gains, w_bf16, kinds, seq_len, rope=None, head_gains=(), gain_ids=None,
             residual=None):
    n_rows = x2d.shape[0]
    n_out = len(kinds)
    gain_ids = tuple(gain_ids) if gain_ids is not None else (0,) * n_out
    assert gains.shape == (max(gain_ids) + 1, D_MODEL) and len(gain_ids) == n_out
    assert w_bf16.shape == (D_MODEL, n_out * D_MODEL)
    blocks_per_seq = seq_len // PROJ_ROWS
    row_spec = pl.BlockSpec((PROJ_ROWS, D_MODEL), lambda i: (i, 0))
    const2 = lambda i: (0, 0)
    resident = lambda shape: pl.BlockSpec(shape, const2, pipeline_mode=pl.Buffered(1))
    in_specs = [row_spec,
                pl.BlockSpec(gains.shape, const2),
                resident((D_MODEL, n_out * D_MODEL))]
    args = [x2d, gains, w_bf16]
    if residual is not None:
        in_specs += [row_spec, resident((D_MODEL, D_MODEL))]
        args += list(residual)
    if rope is not None:
        tab_spec = pl.BlockSpec((PROJ_ROWS, LANES), lambda i: (i % blocks_per_seq, 0))
        in_specs += [pl.BlockSpec((PROJ_COLS, PROJ_COLS), const2), tab_spec, tab_spec, tab_spec]
        args += list(rope)
        for hg in head_gains:
            in_specs.append(pl.BlockSpec((1, LANES), const2))
            args.append(hg)
    out_specs, out_shape = [], []
    for k in kinds:
        if k in ("vt", "vt1"):
            rows = LANES + (ONES_ROWS if k == "vt1" else 0)
            out_specs.append(pl.BlockSpec(
                (1, N_COL_BLOCKS, rows, PROJ_ROWS),
                lambda i: (i // blocks_per_seq, 0, 0, i % blocks_per_seq)))
            out_shape.append(jax.ShapeDtypeStruct(
                (n_rows // seq_len, N_COL_BLOCKS, rows, seq_len), BF16))
        else:
            out_specs.append(row_spec)
            out_shape.append(jax.ShapeDtypeStruct((n_rows, D_MODEL), F32 if k == "f32" else BF16))
    if residual is not None:
        out_specs.append(row_spec)
        out_shape.append(jax.ShapeDtypeStruct((n_rows, D_MODEL), F32))
    return list(pl.pallas_call(
        functools.partial(_in_proj_kernel, kinds=tuple(kinds), gain_ids=gain_ids,
                          residual=residual is not None),
        grid=(n_rows // PROJ_ROWS,),
        in_specs=in_specs,
        out_specs=out_specs,
        out_shape=out_shape,
        compiler_params=pltpu.CompilerParams(
            dimension_semantics=("arbitrary",), vmem_limit_bytes=VMEM_LIMIT),
        name=("res_" if residual is not None else "") + "in_proj_" + "_".join(kinds),
    )(*args))


def _out_proj_kernel(og_ref, w_ref, x_ref, o_ref):
    o_ref[...] = x_ref[...] + jnp.dot(og_ref[...], w_ref[...], preferred_element_type=F32)


def _out_proj(og2d, w_bf16, x2d):
    n_rows = x2d.shape[0]
    row_spec = pl.BlockSpec((PROJ_ROWS, D_MODEL), lambda i: (i, 0))
    return pl.pallas_call(
        _out_proj_kernel,
        grid=(n_rows // PROJ_ROWS,),
        in_specs=[row_spec, pl.BlockSpec((D_MODEL, D_MODEL), lambda i: (0, 0)), row_spec],
        out_specs=row_spec,
        out_shape=jax.ShapeDtypeStruct((n_rows, D_MODEL), F32),
        compiler_params=pltpu.CompilerParams(
            dimension_semantics=("arbitrary",), vmem_limit_bytes=VMEM_LIMIT),
        name="out_proj",
    )(og2d, w_bf16, x2d)


def _stack_halves_t(q):
    qt = q.astype(F32).T.astype(BF16)
    row = lax.broadcasted_iota(jnp.int32, qt.shape, 0)
    zero = jnp.zeros_like(qt)
    return jnp.concatenate(
        [jnp.where(row < HEAD_DIM, qt, zero), jnp.where(row >= HEAD_DIM, qt, zero)], axis=1)


def _silu(g):
    return g * (1.0 / (1.0 + jnp.exp(-g)))


def _diag_positions(tk, tq, key_off=0):
    c = lax.broadcasted_iota(jnp.int32, (tk, 2 * tq), 0) + key_off
    r = lax.broadcasted_iota(jnp.int32, (tk, 2 * tq), 1) & (tq - 1)
    return c, r


def _diff_attn_kernel(q_ref, k_ref, vt_ref, g_ref, sub_ref, lq1_ref, lk1_ref, lq2_ref,
                      lk2_ref, o_ref, acc_scr, s_scr, *, lambda_init):
    tq, tk = DIFF_Q_BLOCK, KEY_BLOCK
    i = pl.program_id(2)
    heads = range(ATT_HEADS)
    cols = [slice(h * LANES, (h + 1) * LANES) for h in heads]
    q2t = [_stack_halves_t(q_ref[0, :, cols[h]]) for h in heads]
    acc_scr[...] = jnp.zeros_like(acc_scr)

    def scores(j, h):
        start = pl.multiple_of(j * tk, tk)
        return jnp.dot(k_ref[0, pl.ds(start, tk), cols[h]], q2t[h],
                       preferred_element_type=F32)

    def step(j, state, diag=None, last=False):
        start = pl.multiple_of(j * tk, tk)
        new_state = []
        for h in heads:
            m_prev, l_prev = state[h]
            s = s_scr[h]
            if diag is not None:
                c, r = _diag_positions(tk, tq, diag * tk)
                s = jnp.where(c <= r, s, -jnp.inf)
            m_new = jnp.maximum(m_prev, jnp.max(s, axis=0, keepdims=True))
            alpha = jnp.exp2(m_prev - m_new)
            p = jnp.exp2(s - m_new).astype(BF16)
            if not last:
                s_scr[h] = scores(j + 1, h)
            pv = jnp.dot(vt_ref[0, h, :, pl.ds(start, tk)], p,
                         preferred_element_type=F32)
            acc_scr[h] = alpha * acc_scr[h] + pv[:LANES]
            new_state.append((m_new, alpha * l_prev + pv[LANES:LANES + 1]))
        return tuple(new_state)

    for h in heads:
        s_scr[h] = scores(0, h)
    m0 = jnp.full((1, 2 * tq), -jnp.inf, F32)
    l0 = jnp.zeros((1, 2 * tq), F32)
    n_diag = tq // tk
    state = lax.fori_loop(0, i * n_diag, lambda j, st: step(j, st), ((m0, l0),) * ATT_HEADS)
    for d in range(n_diag):
        state = step(i * n_diag + d, state, diag=d, last=d == n_diag - 1)

    lam = (jnp.exp(jnp.sum(lq1_ref[...] * lk1_ref[...], axis=-1, keepdims=True))
           - jnp.exp(jnp.sum(lq2_ref[...] * lk2_ref[...], axis=-1, keepdims=True))
           + lambda_init)
    for h in heads:
        acc = acc_scr[h]
        l = state[h][1]
        o = acc[:, :tq] / l[:, :tq] - lam * (acc[:, tq:] / l[:, tq:])
        ms = jnp.mean(o * o, axis=0, keepdims=True)
        on = (o * lax.rsqrt(ms + RMS_EPS)).T
        o_ref[0, :, cols[h]] = (on * sub_ref[...] * (1.0 - lambda_init)
                                * _silu(g_ref[0, :, cols[h]])).astype(BF16)


def _diff_attention(q, k, vt, g, subln, lq1, lk1, lq2, lk2, lambda_init):
    b, s, _ = q.shape
    t = DIFF_Q_BLOCK
    w = ATT_HEADS * LANES
    q_spec = pl.BlockSpec((1, t, w), lambda bi, h, i: (bi, i, h))
    k_spec = pl.BlockSpec((1, s, w), lambda bi, h, i: (bi, 0, h),
                          pipeline_mode=pl.Buffered(1))
    vt_spec = pl.BlockSpec((1, ATT_HEADS, LANES + ONES_ROWS, s), lambda bi, h, i: (bi, h, 0, 0),
                           pipeline_mode=pl.Buffered(1))
    vec = lambda n: pl.BlockSpec((1, n), lambda bi, h, i: (0, 0))
    return pl.pallas_call(
        functools.partial(_diff_attn_kernel, lambda_init=lambda_init),
        grid=(b, N_COL_BLOCKS // ATT_HEADS, s // t),
        in_specs=[q_spec, k_spec, vt_spec, q_spec, vec(LANES),
                  vec(HEAD_DIM), vec(HEAD_DIM), vec(HEAD_DIM), vec(HEAD_DIM)],
        out_specs=q_spec,
        out_shape=jax.ShapeDtypeStruct((b, s, D_MODEL), BF16),
        scratch_shapes=[pltpu.VMEM((ATT_HEADS, LANES, 2 * t), F32),
                        pltpu.VMEM((ATT_HEADS, KEY_BLOCK, 2 * t), F32)],
        compiler_params=pltpu.CompilerParams(
            dimension_semantics=("arbitrary", "arbitrary", "arbitrary"),
            vmem_limit_bytes=VMEM_LIMIT),
        name="diff_attention",
    )(q, k, vt, g, subln.reshape(1, LANES), lq1.reshape(1, HEAD_DIM), lk1.reshape(1, HEAD_DIM),
      lq2.reshape(1, HEAD_DIM), lk2.reshape(1, HEAD_DIM))


def _sb_attn_kernel(q_ref, k_ref, vt_ref, g_ref, tri_ref, o_ref, acc_scr):
    t = KEY_BLOCK
    i = pl.program_id(2)
    pairs = range(SB_PAIRS)
    cols = [slice(h * LANES, (h + 1) * LANES) for h in pairs]
    q2t = [_stack_halves_t(q_ref[0, :, cols[h]]) for h in pairs]
    acc_scr[...] = jnp.zeros_like(acc_scr)

    def step(j, carry, masked):
        start = pl.multiple_of(j * t, t)
        tri = tri_ref[...]
        zs = [jnp.dot(k_ref[0, pl.ds(start, t), cols[h]], q2t[h],
                      preferred_element_type=F32) for h in pairs]
        if masked:
            c, r = _diag_positions(t, t)
            keep = c < r
        sps = []
        for h in pairs:
            z = zs[h]
            sp = jnp.maximum(z, jnp.log2(1.0 + jnp.exp2(jnp.minimum(z, SOFTPLUS_CLAMP))))
            if masked:
                sp = jnp.where(keep, sp, 0.0)
            sps.append(sp.astype(BF16))
        incls = [jnp.dot(tri, sps[h], preferred_element_type=F32) for h in pairs]
        probs = []
        for h in pairs:
            a = jnp.exp2(zs[h] - incls[h] - carry[h])
            if masked:
                a = jnp.where(keep, a, 0.0)
            probs.append(a.astype(BF16))
        pvs = []
        for h in pairs:
            vt = vt_ref[0, h, :, pl.ds(start, t)]
            pvs.append(jnp.concatenate(
                [jnp.dot(vt[:HEAD_DIM], probs[h][:, :t], preferred_element_type=F32),
                 jnp.dot(vt[HEAD_DIM:], probs[h][:, t:], preferred_element_type=F32)], axis=0))
        for h in pairs:
            acc_scr[h] += pvs[h]
        return tuple(carry[h] + incls[h][0:1, :] for h in pairs)

    def smallest(carry):
        return jnp.min(functools.reduce(jnp.minimum, carry))

    zero = jnp.zeros((1, 2 * t), F32)
    carry = step(i, (zero,) * SB_PAIRS, True)

    def keep_walking(st):
        jj, _, low = st
        return jnp.logical_and(jj < i, low < SB_DEAD_CARRY)

    def walk(st):
        jj, c, _ = st
        c = step(i - 1 - jj, c, False)
        return jj + 1, c, smallest(c)

    lax.while_loop(keep_walking, walk, (jnp.int32(0), carry, smallest(carry)))

    for h in pairs:
        o = acc_scr[h].T
        o_ref[0, :, cols[h]] = (o * _silu(g_ref[0, :, cols[h]])).astype(BF16)


def _sb_attention(q, k, vt, g, tri):
    b, s, _ = q.shape
    t = KEY_BLOCK
    w = SB_PAIRS * LANES
    q_spec = pl.BlockSpec((1, t, w), lambda bi, h, i: (bi, i, h))
    k_spec = pl.BlockSpec((1, s, w), lambda bi, h, i: (bi, 0, h),
                          pipeline_mode=pl.Buffered(1))
    vt_spec = pl.BlockSpec((1, SB_PAIRS, LANES, s), lambda bi, h, i: (bi, h, 0, 0),
                           pipeline_mode=pl.Buffered(1))
    return pl.pallas_call(
        _sb_attn_kernel,
        grid=(b, N_COL_BLOCKS // SB_PAIRS, s // t),
        in_specs=[q_spec, k_spec, vt_spec, q_spec,
                  pl.BlockSpec((t, t), lambda bi, h, i: (0, 0))],
        out_specs=q_spec,
        out_shape=jax.ShapeDtypeStruct((b, s, D_MODEL), BF16),
        scratch_shapes=[pltpu.VMEM((SB_PAIRS, LANES, t), F32)],
        compiler_params=pltpu.CompilerParams(
            dimension_semantics=("arbitrary", "arbitrary", "arbitrary"),
            vmem_limit_bytes=VMEM_LIMIT),
        name="sb_attention",
    )(q, k, vt, g, tri)


def _rope_tables(seq_len):
    half = ROT_DIM // 2
    lane = jnp.arange(LANES) % HEAD_DIM
    inv = jnp.where(lane < ROT_DIM,
                    jnp.power(jnp.float32(ROPE_THETA), -(lane % half).astype(F32) / half), 0.0)
    ang = jnp.arange(seq_len).astype(F32)[:, None] * inv[None, :]
    cos, sin = jnp.cos(ang), jnp.sin(ang)
    return (cos, jnp.where(lane < half, -sin, 0.0),
            jnp.where((lane >= half) & (lane < ROT_DIM), sin, 0.0))


def kernel(x, a_norm, a_w_in, a_w_out, a_q_norm, a_k_norm, a_lq1, a_lk1, a_lq2, a_lk2,
           a_subln, kv_norm, w_kv, b_norm, b_w_in, b_w_out):
    b, s, d = x.shape
    assert d == D_MODEL and s % PROJ_ROWS == 0 and s % DIFF_Q_BLOCK == 0
    assert DIFF_Q_BLOCK % KEY_BLOCK == 0
    n = b * s
    shp = (b, s, d)
    x2 = x.reshape(n, d)

    sub = jnp.arange(PROJ_COLS) // HEAD_DIM
    block_diag = (sub[:, None] == sub[None, :]).astype(BF16)
    idx = jnp.arange(KEY_BLOCK)
    tri = (idx[None, :] >= idx[:, None]).astype(BF16)
    rope = (block_diag,) + _rope_tables(s)
    tile2 = lambda v: jnp.concatenate([v, v]).reshape(1, LANES)

    pending = None
    for layer in range(N_A_LAYERS):
        outs = _in_proj(
            x2, a_norm[layer][None], a_w_in[layer].astype(BF16), ("qk", "qk", "vt1", "f32"), s,
            rope=rope,
            head_gains=(tile2(a_q_norm[layer]) * Q_SCALE, tile2(a_k_norm[layer])),
            residual=pending)
        if pending is not None:
            x2 = outs.pop()
        q, k, vt, g = outs
        og = _diff_attention(q.reshape(shp), k.reshape(shp), vt, g.reshape(shp),
                             a_subln[layer], a_lq1[layer], a_lk1[layer], a_lq2[layer],
                             a_lk2[layer], LAMBDA_INIT[layer])
        pending = (og.reshape(n, d), a_w_out[layer].astype(BF16))

    for j in range(N_B_LAYERS):
        if j == 0:
            outs = _in_proj(
                x2, jnp.stack([kv_norm, b_norm[0]]),
                jnp.concatenate([w_kv.astype(BF16), b_w_in[0].astype(BF16)], axis=1),
                ("bf16", "vt", "q_scaled", "f32"), s, gain_ids=(0, 0, 1, 1), residual=pending)
            x2 = outs.pop()
            sk, svt, q, g = outs
        else:
            q, g, x2 = _in_proj(x2, b_norm[j][None], b_w_in[j].astype(BF16),
                                ("q_scaled", "f32"), s, residual=pending)
        og = _sb_attention(q.reshape(shp), sk.reshape(shp), svt, g.reshape(shp), tri)
        pending = (og.reshape(n, d), b_w_out[j].astype(BF16))
    return _out_proj(*pending, x2).reshape(b, s, d)
```

```python
import functools
import math

import jax
import jax.numpy as jnp
from jax import lax
from jax.experimental import pallas as pl
from jax.experimental.pallas import tpu as pltpu

D_MODEL = 1024
HEAD_DIM = 64
LANES = 128
N_COL_BLOCKS = D_MODEL // LANES
N_A_LAYERS = 2
N_B_LAYERS = 2
LAMBDA_INIT = tuple(0.8 - 0.6 * math.exp(-0.3 * l) for l in range(N_A_LAYERS))
ROPE_THETA = 500000.0
ROT_DIM = HEAD_DIM // 4
RMS_EPS = 1e-6
LOG2E = 1.4426950408889634
SOFTPLUS_CLAMP = 64.0
SB_DEAD_CARRY = 160.0
Q_SCALE = HEAD_DIM ** -0.5 * LOG2E

PROJ_ROWS = 512
PROJ_COLS = 256
KEY_BLOCK = 256
DIFF_Q_BLOCK = 256
ATT_HEADS = 8
ONES_ROWS = 16
SB_PAIRS = 8
VMEM_LIMIT = 56 * 1024 * 1024

F32 = jnp.float32
BF16 = jnp.bfloat16


def _in_proj_kernel(*refs, kinds, gain_ids, residual):
    n_qk = sum(k == "qk" for k in kinds)
    x_ref, gain_ref, w_ref = refs[:3]
    pos = 3
    if residual:
        og_ref, w_out_ref = refs[pos:pos + 2]
        pos += 2
    if n_qk:
        bd_ref, cos_ref, sa_ref, sb_ref = refs[pos:pos + 4]
        pos += 4
        head_gain_refs = refs[pos:pos + n_qk]
        pos += n_qk
    out_refs = refs[pos:]

    x = x_ref[...]
    if residual:
        x = x + jnp.dot(og_ref[...], w_out_ref[...], preferred_element_type=F32)
        out_refs[len(kinds)][...] = x
    ms = jnp.mean(x * x, axis=-1, keepdims=True)
    xr = x * lax.rsqrt(ms + RMS_EPS)
    xn = {g: (xr * gain_ref[g:g + 1, :]).astype(BF16) for g in sorted(set(gain_ids))}

    head_gains = {}
    for gi, kind in enumerate(kinds):
        if kind == "qk":
            head_gains[gi] = head_gain_refs[len(head_gains)][...]

    def finish(t, gi, c):
        kind, out_ref = kinds[gi], out_refs[gi]
        if kind == "qk":
            ss = jnp.dot((t * t).astype(BF16), bd_ref[...], preferred_element_type=F32)
            tn = t * lax.rsqrt(ss * (1.0 / HEAD_DIM) + RMS_EPS)
            for h in range(PROJ_COLS // LANES):
                th = tn[:, h * LANES:(h + 1) * LANES] * head_gains[gi]
                rot = (th * cos_ref[...]
                       + pltpu.roll(th, LANES - ROT_DIM // 2, 1) * sa_ref[...]
                       + pltpu.roll(th, ROT_DIM // 2, 1) * sb_ref[...])
                lo = c + h * LANES
                out_ref[:, lo:lo + LANES] = rot.astype(BF16)
        elif kind in ("vt", "vt1"):
            for h in range(PROJ_COLS // LANES):
                blk = (c + h * LANES) // LANES
                out_ref[0, blk, 0:LANES, :] = t[:, h * LANES:(h + 1) * LANES].T.astype(BF16)
                if kind == "vt1":
                    out_ref[0, blk, LANES:LANES + ONES_ROWS, :] = jnp.ones(
                        (ONES_ROWS, t.shape[0]), BF16)
        elif kind == "bf16":
            out_ref[:, c:c + PROJ_COLS] = t.astype(BF16)
        elif kind == "q_scaled":
            out_ref[:, c:c + PROJ_COLS] = (t * Q_SCALE).astype(BF16)
        else:
            out_ref[:, c:c + PROJ_COLS] = t

    def project(gi, c):
        col = gi * D_MODEL + c
        return jnp.dot(xn[gain_ids[gi]], w_ref[:, col:col + PROJ_COLS],
                       preferred_element_type=F32)

    chunks = [(gi, c) for gi in range(len(kinds)) for c in range(0, D_MODEL, PROJ_COLS)]
    t_next = project(*chunks[0])
    for n, (gi, c) in enumerate(chunks):
        t = t_next
        if n + 1 < len(chunks):
            t_next = project(*chunks[n + 1])
        finish(t, gi, c)


def _in_proj(x2d, gains, w_bf16, kinds, seq_len, rope=None, head_gains=(), gain_ids=None,
             residual=None):
    n_rows = x2d.shape[0]
    n_out = len(kinds)
    gain_ids = tuple(gain_ids) if gain_ids is not None else (0,) * n_out
    assert gains.shape == (max(gain_ids) + 1, D_MODEL) and len(gain_ids) == n_out
    assert w_bf16.shape == (D_MODEL, n_out * D_MODEL)
    blocks_per_seq = seq_len // PROJ_ROWS
    row_spec = pl.BlockSpec((PROJ_ROWS, D_MODEL), lambda i: (i, 0))
    const2 = lambda i: (0, 0)
    resident = lambda shape: pl.BlockSpec(shape, const2, pipeline_mode=pl.Buffered(1))
    in_specs = [row_spec,
                pl.BlockSpec(gains.shape, const2),
                resident((D_MODEL, n_out * D_MODEL))]
    args = [x2d, gains, w_bf16]
    if residual is not None:
        in_specs += [row_spec, resident((D_MODEL, D_MODEL))]
        args += list(residual)
    if rope is not None:
        tab_spec = pl.BlockSpec((PROJ_ROWS, LANES), lambda i: (i % blocks_per_seq, 0))
        in_specs += [pl.BlockSpec((PROJ_COLS, PROJ_COLS), const2), tab_spec, tab_spec, tab_spec]
        args += list(rope)
        for hg in head_gains:
            in_specs.append(pl.BlockSpec((1, LANES), const2))
            args.append(hg)
    out_specs, out_shape = [], []
    for k in kinds:
        if k in ("vt", "vt1"):
            rows = LANES + (ONES_ROWS if k == "vt1" else 0)
            out_specs.append(pl.BlockSpec(
                (1, N_COL_BLOCKS, rows, PROJ_ROWS),
                lambda i: (i // blocks_per_seq, 0, 0, i % blocks_per_seq)))
            out_shape.append(jax.ShapeDtypeStruct(
                (n_rows // seq_len, N_COL_BLOCKS, rows, seq_len), BF16))
        else:
            out_specs.append(row_spec)
            out_shape.append(jax.ShapeDtypeStruct((n_rows, D_MODEL), F32 if k == "f32" else BF16))
    if residual is not None:
        out_specs.append(row_spec)
        out_shape.append(jax.ShapeDtypeStruct((n_rows, D_MODEL), F32))
    return list(pl.pallas_call(
        functools.partial(_in_proj_kernel, kinds=tuple(kinds), gain_ids=gain_ids,
                          residual=residual is not None),
        grid=(n_rows // PROJ_ROWS,),
        in_specs=in_specs,
        out_specs=out_specs,
        out_shape=out_shape,
        compiler_params=pltpu.CompilerParams(
            dimension_semantics=("arbitrary",), vmem_limit_bytes=VMEM_LIMIT),
        name=("res_" if residual is not None else "") + "in_proj_" + "_".join(kinds),
    )(*args))


def _out_proj_kernel(og_ref, w_ref, x_ref, o_ref):
    o_ref[...] = x_ref[...] + jnp.dot(og_ref[...], w_ref[...], preferred_element_type=F32)


def _out_proj(og2d, w_bf16, x2d):
    n_rows = x2d.shape[0]
    row_spec = pl.BlockSpec((PROJ_ROWS, D_MODEL), lambda i: (i, 0))
    return pl.pallas_call(
        _out_proj_kernel,
        grid=(n_rows // PROJ_ROWS,),
        in_specs=[row_spec, pl.BlockSpec((D_MODEL, D_MODEL), lambda i: (0, 0)), row_spec],
        out_specs=row_spec,
        out_shape=jax.ShapeDtypeStruct((n_rows, D_MODEL), F32),
        compiler_params=pltpu.CompilerParams(
            dimension_semantics=("arbitrary",), vmem_limit_bytes=VMEM_LIMIT),
        name="out_proj",
    )(og2d, w_bf16, x2d)


def _stack_halves_t(q):
    qt = q.astype(F32).T.astype(BF16)
    row = lax.broadcasted_iota(jnp.int32, qt.shape, 0)
    zero = jnp.zeros_like(qt)
    return jnp.concatenate(
        [jnp.where(row < HEAD_DIM, qt, zero), jnp.where(row >= HEAD_DIM, qt, zero)], axis=1)


def _silu(g):
    return g * (1.0 / (1.0 + jnp.exp(-g)))


def _diag_positions(tk, tq, key_off=0):
    c = lax.broadcasted_iota(jnp.int32, (tk, 2 * tq), 0) + key_off
    r = lax.broadcasted_iota(jnp.int32, (tk, 2 * tq), 1) & (tq - 1)
    return c, r


def _diff_attn_kernel(q_ref, k_ref, vt_ref, g_ref, sub_ref, lq1_ref, lk1_ref, lq2_ref,
                      lk2_ref, o_ref, acc_scr, s_scr, *, lambda_init):
    tq, tk = DIFF_Q_BLOCK, KEY_BLOCK
    i = pl.program_id(2)
    heads = range(ATT_HEADS)
    cols = [slice(h * LANES, (h + 1) * LANES) for h in heads]
    q2t = [_stack_halves_t(q_ref[0, :, cols[h]]) for h in heads]
    acc_scr[...] = jnp.zeros_like(acc_scr)

    def scores(j, h):
        start = pl.multiple_of(j * tk, tk)
        return jnp.dot(k_ref[0, pl.ds(start, tk), cols[h]], q2t[h],
                       preferred_element_type=F32)

    def step(j, state, diag=None, last=False):
        start = pl.multiple_of(j * tk, tk)
        new_state = []
        for h in heads:
            m_prev, l_prev = state[h]
            s = s_scr[h]
            if diag is not None:
                c, r = _diag_positions(tk, tq, diag * tk)
                s = jnp.where(c <= r, s, -jnp.inf)
            m_new = jnp.maximum(m_prev, jnp.max(s, axis=0, keepdims=True))
            alpha = jnp.exp2(m_prev - m_new)
            p = jnp.exp2(s - m_new).astype(BF16)
            if not last:
                s_scr[h] = scores(j + 1, h)
            pv = jnp.dot(vt_ref[0, h, :, pl.ds(start, tk)], p,
                         preferred_element_type=F32)
            acc_scr[h] = alpha * acc_scr[h] + pv[:LANES]
            new_state.append((m_new, alpha * l_prev + pv[LANES:LANES + 1]))
        return tuple(new_state)

    for h in heads:
        s_scr[h] = scores(0, h)
    m0 = jnp.full((1, 2 * tq), -jnp.inf, F32)
    l0 = jnp.zeros((1, 2 * tq), F32)
    n_diag = tq // tk
    state = lax.fori_loop(0, i * n_diag, lambda j, st: step(j, st), ((m0, l0),) * ATT_HEADS)
    for d in range(n_diag):
        state = step(i * n_diag + d, state, diag=d, last=d == n_diag - 1)

    lam = (jnp.exp(jnp.sum(lq1_ref[...] * lk1_ref[...], axis=-1, keepdims=True))
           - jnp.exp(jnp.sum(lq2_ref[...] * lk2_ref[...], axis=-1, keepdims=True))
           + lambda_init)
    for h in heads:
        acc = acc_scr[h]
        l = state[h][1]
        o = acc[:, :tq] / l[:, :tq] - lam * (acc[:, tq:] / l[:, tq:])
        ms = jnp.mean(o * o, axis=0, keepdims=True)
        on = (o * lax.rsqrt(ms + RMS_EPS)).T
        o_ref[0, :, cols[h]] = (on * sub_ref[...] * (1.0 - lambda_init)
                                * _silu(g_ref[0, :, cols[h]])).astype(BF16)


def _diff_attention(q, k, vt, g, subln, lq1, lk1, lq2, lk2, lambda_init):
    b, s, _ = q.shape
    t = DIFF_Q_BLOCK
    w = ATT_HEADS * LANES
    q_spec = pl.BlockSpec((1, t, w), lambda bi, h, i: (bi, i, h))
    k_spec = pl.BlockSpec((1, s, w), lambda bi, h, i: (bi, 0, h),
                          pipeline_mode=pl.Buffered(1))
    vt_spec = pl.BlockSpec((1, ATT_HEADS, LANES + ONES_ROWS, s), lambda bi, h, i: (bi, h, 0, 0),
                           pipeline_mode=pl.Buffered(1))
    vec = lambda n: pl.BlockSpec((1, n), lambda bi, h, i: (0, 0))
    return pl.pallas_call(
        functools.partial(_diff_attn_kernel, lambda_init=lambda_init),
        grid=(b, N_COL_BLOCKS // ATT_HEADS, s // t),
        in_specs=[q_spec, k_spec, vt_spec, q_spec, vec(LANES),
                  vec(HEAD_DIM), vec(HEAD_DIM), vec(HEAD_DIM), vec(HEAD_DIM)],
        out_specs=q_spec,
        out_shape=jax.ShapeDtypeStruct((b, s, D_MODEL), BF16),
        scratch_shapes=[pltpu.VMEM((ATT_HEADS, LANES, 2 * t), F32),
                        pltpu.VMEM((ATT_HEADS, KEY_BLOCK, 2 * t), F32)],
        compiler_params=pltpu.CompilerParams(
            dimension_semantics=("arbitrary", "arbitrary", "arbitrary"),
            vmem_limit_bytes=VMEM_LIMIT),
        name="diff_attention",
    )(q, k, vt, g, subln.reshape(1, LANES), lq1.reshape(1, HEAD_DIM), lk1.reshape(1, HEAD_DIM),
      lq2.reshape(1, HEAD_DIM), lk2.reshape(1, HEAD_DIM))


def _sb_attn_kernel(q_ref, k_ref, vt_ref, g_ref, tri_ref, o_ref, acc_scr):
    t = KEY_BLOCK
    i = pl.program_id(2)
    pairs = range(SB_PAIRS)
    cols = [slice(h * LANES, (h + 1) * LANES) for h in pairs]
    q2t = [_stack_halves_t(q_ref[0, :, cols[h]]) for h in pairs]
    acc_scr[...] = jnp.zeros_like(acc_scr)

    def step(j, carry, masked):
        start = pl.multiple_of(j * t, t)
        tri = tri_ref[...]
        zs = [jnp.dot(k_ref[0, pl.ds(start, t), cols[h]], q2t[h],
                      preferred_element_type=F32) for h in pairs]
        if masked:
            c, r = _diag_positions(t, t)
            keep = c < r
        sps, log_betas = [], []
        for h in pairs:
            z = zs[h]
            sp = jnp.maximum(z, jnp.log2(1.0 + jnp.exp2(jnp.minimum(z, SOFTPLUS_CLAMP))))
            log_betas.append(z - sp)
            if masked:
                sp = jnp.where(keep, sp, 0.0)
            sps.append(sp.astype(BF16))
        excls = [jnp.dot(tri, sps[h], preferred_element_type=F32) for h in pairs]
        probs = []
        for h in pairs:
            a = jnp.exp2(log_betas[h] - excls[h] - carry[h])
            if masked:
                a = jnp.where(keep, a, 0.0)
            probs.append(a.astype(BF16))
        pvs = []
        for h in pairs:
            vt = vt_ref[0, h, :, pl.ds(start, t)]
            pvs.append(jnp.concatenate(
                [jnp.dot(vt[:HEAD_DIM], probs[h][:, :t], preferred_element_type=F32),
                 jnp.dot(vt[HEAD_DIM:], probs[h][:, t:], preferred_element_type=F32)], axis=0))
        for h in pairs:
            acc_scr[h] += pvs[h]
        return tuple(carry[h] + excls[h][0:1, :] + sps[h][0:1, :].astype(F32) for h in pairs)

    def smallest(carry):
        return jnp.min(functools.reduce(jnp.minimum, carry))

    zero = jnp.zeros((1, 2 * t), F32)
    carry = step(i, (zero,) * SB_PAIRS, True)

    def keep_walking(st):
        jj, _, low = st
        return jnp.logical_and(jj < i, low < SB_DEAD_CARRY)

    def walk(st):
        jj, c, _ = st
        c = step(i - 1 - jj, c, False)
        return jj + 1, c, smallest(c)

    lax.while_loop(keep_walking, walk, (jnp.int32(0), carry, smallest(carry)))

    for h in pairs:
        o = acc_scr[h].T
        o_ref[0, :, cols[h]] = (o * _silu(g_ref[0, :, cols[h]])).astype(BF16)


def _sb_attention(q, k, vt, g, tri):
    b, s, _ = q.shape
    t = KEY_BLOCK
    w = SB_PAIRS * LANES
    q_spec = pl.BlockSpec((1, t, w), lambda bi, h, i: (bi, i, h))
    k_spec = pl.BlockSpec((1, s, w), lambda bi, h, i: (bi, 0, h),
                          pipeline_mode=pl.Buffered(1))
    vt_spec = pl.BlockSpec((1, SB_PAIRS, LANES, s), lambda bi, h, i: (bi, h, 0, 0),
                           pipeline_mode=pl.Buffered(1))
    return pl.pallas_call(
        _sb_attn_kernel,
        grid=(b, N_COL_BLOCKS // SB_PAIRS, s // t),
        in_specs=[q_spec, k_spec, vt_spec, q_spec,
                  pl.BlockSpec((t, t), lambda bi, h, i: (0, 0))],
        out_specs=q_spec,
        out_shape=jax.ShapeDtypeStruct((b, s, D_MODEL), BF16),
        scratch_shapes=[pltpu.VMEM((SB_PAIRS, LANES, t), F32)],
        compiler_params=pltpu.CompilerParams(
            dimension_semantics=("arbitrary", "arbitrary", "arbitrary"),
            vmem_limit_bytes=VMEM_LIMIT),
        name="sb_attention",
    )(q, k, vt, g, tri)


def _rope_tables(seq_len):
    half = ROT_DIM // 2
    lane = jnp.arange(LANES) % HEAD_DIM
    inv = jnp.where(lane < ROT_DIM,
                    jnp.power(jnp.float32(ROPE_THETA), -(lane % half).astype(F32) / half), 0.0)
    ang = jnp.arange(seq_len).astype(F32)[:, None] * inv[None, :]
    cos, sin = jnp.cos(ang), jnp.sin(ang)
    return (cos, jnp.where(lane < half, -sin, 0.0),
            jnp.where((lane >= half) & (lane < ROT_DIM), sin, 0.0))


def kernel(x, a_norm, a_w_in, a_w_out, a_q_norm, a_k_norm, a_lq1, a_lk1, a_lq2, a_lk2,
           a_subln, kv_norm, w_kv, b_norm, b_w_in, b_w_out):
    b, s, d = x.shape
    assert d == D_MODEL and s % PROJ_ROWS == 0 and s % DIFF_Q_BLOCK == 0
    assert DIFF_Q_BLOCK % KEY_BLOCK == 0
    n = b * s
    shp = (b, s, d)
    x2 = x.reshape(n, d)

    sub = jnp.arange(PROJ_COLS) // HEAD_DIM
    block_diag = (sub[:, None] == sub[None, :]).astype(BF16)
    idx = jnp.arange(KEY_BLOCK)
    tri = (idx[None, :] > idx[:, None]).astype(BF16)
    rope = (block_diag,) + _rope_tables(s)
    tile2 = lambda v: jnp.concatenate([v, v]).reshape(1, LANES)

    pending = None
    for layer in range(N_A_LAYERS):
        outs = _in_proj(
            x2, a_norm[layer][None], a_w_in[layer].astype(BF16), ("qk", "qk", "vt1", "f32"), s,
            rope=rope,
            head_gains=(tile2(a_q_norm[layer]) * Q_SCALE, tile2(a_k_norm[layer])),
            residual=pending)
        if pending is not None:
            x2 = outs.pop()
        q, k, vt, g = outs
        og = _diff_attention(q.reshape(shp), k.reshape(shp), vt, g.reshape(shp),
                             a_subln[layer], a_lq1[layer], a_lk1[layer], a_lq2[layer],
                             a_lk2[layer], LAMBDA_INIT[layer])
        pending = (og.reshape(n, d), a_w_out[layer].astype(BF16))

    for j in range(N_B_LAYERS):
        if j == 0:
            outs = _in_proj(
                x2, jnp.stack([kv_norm, b_norm[0]]),
                jnp.concatenate([w_kv.astype(BF16), b_w_in[0].astype(BF16)], axis=1),
                ("bf16", "vt", "q_scaled", "f32"), s, gain_ids=(0, 0, 1, 1), residual=pending)
            x2 = outs.pop()
            sk, svt, q, g = outs
        else:
            q, g, x2 = _in_proj(x2, b_norm[j][None], b_w_in[j].astype(BF16),
                                ("q_scaled", "f32"), s, residual=pending)
        og = _sb_attention(q.reshape(shp), sk.reshape(shp), svt, g.reshape(shp), tri)
        pending = (og.reshape(n, d), b_w_out[j].astype(BF16))
    return _out_proj(*pending, x2).reshape(b, s, d)
```

```python
import functools
import math

import jax
import jax.numpy as jnp
from jax import lax
from jax.experimental import pallas as pl
from jax.experimental.pallas import tpu as pltpu

D_MODEL = 1024
HEAD_DIM = 64
LANES = 128
N_COL_BLOCKS = D_MODEL // LANES
N_A_LAYERS = 2
N_B_LAYERS = 2
LAMBDA_INIT = tuple(0.8 - 0.6 * math.exp(-0.3 * l) for l in range(N_A_LAYERS))
ROPE_THETA = 500000.0
ROT_DIM = HEAD_DIM // 4
RMS_EPS = 1e-6
LOG2E = 1.4426950408889634
SOFTPLUS_CLAMP = 64.0
SB_DEAD_CARRY = 160.0
Q_SCALE = HEAD_DIM ** -0.5 * LOG2E

PROJ_ROWS = 512
PROJ_COLS = 256
KEY_BLOCK = 256
DIFF_Q_BLOCK = 256
ATT_HEADS = 8
DIFF_UNROLL = 4
ONES_ROWS = 16
SB_PAIRS = 8
VMEM_LIMIT = 56 * 1024 * 1024

F32 = jnp.float32
BF16 = jnp.bfloat16


def _in_proj_kernel(*refs, kinds, gain_ids, residual):
    n_qk = sum(k == "qk" for k in kinds)
    x_ref, gain_ref, w_ref = refs[:3]
    pos = 3
    if residual:
        og_ref, w_out_ref = refs[pos:pos + 2]
        pos += 2
    if n_qk:
        bd_ref, cos_ref, sa_ref, sb_ref = refs[pos:pos + 4]
        pos += 4
        head_gain_refs = refs[pos:pos + n_qk]
        pos += n_qk
    out_refs = refs[pos:]

    x = x_ref[...]
    if residual:
        x = x + jnp.dot(og_ref[...], w_out_ref[...], preferred_element_type=F32)
        out_refs[len(kinds)][...] = x
    ms = jnp.mean(x * x, axis=-1, keepdims=True)
    xr = x * lax.rsqrt(ms + RMS_EPS)
    xn = {g: (xr * gain_ref[g:g + 1, :]).astype(BF16) for g in sorted(set(gain_ids))}

    head_gains = {}
    for gi, kind in enumerate(kinds):
        if kind == "qk":
            head_gains[gi] = head_gain_refs[len(head_gains)][...]

    def finish(t, gi, c):
        kind, out_ref = kinds[gi], out_refs[gi]
        if kind == "qk":
            ss = jnp.dot((t * t).astype(BF16), bd_ref[...], preferred_element_type=F32)
            tn = t * lax.rsqrt(ss * (1.0 / HEAD_DIM) + RMS_EPS)
            for h in range(PROJ_COLS // LANES):
                th = tn[:, h * LANES:(h + 1) * LANES] * head_gains[gi]
                rot = (th * cos_ref[...]
                       + pltpu.roll(th, LANES - ROT_DIM // 2, 1) * sa_ref[...]
                       + pltpu.roll(th, ROT_DIM // 2, 1) * sb_ref[...])
                lo = c + h * LANES
                out_ref[:, lo:lo + LANES] = rot.astype(BF16)
        elif kind in ("vt", "vt1"):
            for h in range(PROJ_COLS // LANES):
                blk = (c + h * LANES) // LANES
                out_ref[0, blk, 0:LANES, :] = t[:, h * LANES:(h + 1) * LANES].T.astype(BF16)
                if kind == "vt1":
                    out_ref[0, blk, LANES:LANES + ONES_ROWS, :] = jnp.ones(
                        (ONES_ROWS, t.shape[0]), BF16)
        elif kind == "bf16":
            out_ref[:, c:c + PROJ_COLS] = t.astype(BF16)
        elif kind == "q_scaled":
            out_ref[:, c:c + PROJ_COLS] = (t * Q_SCALE).astype(BF16)
        else:
            out_ref[:, c:c + PROJ_COLS] = t

    def project(gi, c):
        col = gi * D_MODEL + c
        return jnp.dot(xn[gain_ids[gi]], w_ref[:, col:col + PROJ_COLS],
                       preferred_element_type=F32)

    chunks = [(gi, c) for gi in range(len(kinds)) for c in range(0, D_MODEL, PROJ_COLS)]
    t_next = project(*chunks[0])
    for n, (gi, c) in enumerate(chunks):
        t = t_next
        if n + 1 < len(chunks):
            t_next = project(*chunks[n + 1])
        finish(t, gi, c)


def _in_proj(x2d, gains, w_bf16, kinds, seq_len, rope=None, head_gains=(), gain_ids=None,
             residual=None):
    n_rows = x2d.shape[0]
    n_out = len(kinds)
    gain_ids = tuple(gain_ids) if gain_ids is not None else (0,) * n_out
    assert gains.shape == (max(gain_ids) + 1, D_MODEL) and len(gain_ids) == n_out
    assert w_bf16.shape == (D_MODEL, n_out * D_MODEL)
    blocks_per_seq = seq_len // PROJ_ROWS
    row_spec = pl.BlockSpec((PROJ_ROWS, D_MODEL), lambda i: (i, 0))
    const2 = lambda i: (0, 0)
    resident = lambda shape: pl.BlockSpec(shape, const2, pipeline_mode=pl.Buffered(1))
    in_specs = [row_spec,
                pl.BlockSpec(gains.shape, const2),
                resident((D_MODEL, n_out * D_MODEL))]
    args = [x2d, gains, w_bf16]
    if residual is not None:
        in_specs += [row_spec, resident((D_MODEL, D_MODEL))]
        args += list(residual)
    if rope is not None:
        tab_spec = pl.BlockSpec((PROJ_ROWS, LANES), lambda i: (i % blocks_per_seq, 0))
        in_specs += [pl.BlockSpec((PROJ_COLS, PROJ_COLS), const2), tab_spec, tab_spec, tab_spec]
        args += list(rope)
        for hg in head_gains:
            in_specs.append(pl.BlockSpec((1, LANES), const2))
            args.append(hg)
    out_specs, out_shape = [], []
    for k in kinds:
        if k in ("vt", "vt1"):
            rows = LANES + (ONES_ROWS if k == "vt1" else 0)
            out_specs.append(pl.BlockSpec(
                (1, N_COL_BLOCKS, rows, PROJ_ROWS),
                lambda i: (i // blocks_per_seq, 0, 0, i % blocks_per_seq)))
            out_shape.append(jax.ShapeDtypeStruct(
                (n_rows // seq_len, N_COL_BLOCKS, rows, seq_len), BF16))
        else:
            out_specs.append(row_spec)
            out_shape.append(jax.ShapeDtypeStruct((n_rows, D_MODEL), F32 if k == "f32" else BF16))
    if residual is not None:
        out_specs.append(row_spec)
        out_shape.append(jax.ShapeDtypeStruct((n_rows, D_MODEL), F32))
    return list(pl.pallas_call(
        functools.partial(_in_proj_kernel, kinds=tuple(kinds), gain_ids=gain_ids,
                          residual=residual is not None),
        grid=(n_rows // PROJ_ROWS,),
        in_specs=in_specs,
        out_specs=out_specs,
        out_shape=out_shape,
        compiler_params=pltpu.CompilerParams(
            dimension_semantics=("arbitrary",), vmem_limit_bytes=VMEM_LIMIT),
        name=("res_" if residual is not None else "") + "in_proj_" + "_".join(kinds),
    )(*args))


def _out_proj_kernel(og_ref, w_ref, x_ref, o_ref):
    o_ref[...] = x_ref[...] + jnp.dot(og_ref[...], w_ref[...], preferred_element_type=F32)


def _out_proj(og2d, w_bf16, x2d):
    n_rows = x2d.shape[0]
    row_spec = pl.BlockSpec((PROJ_ROWS, D_MODEL), lambda i: (i, 0))
    return pl.pallas_call(
        _out_proj_kernel,
        grid=(n_rows // PROJ_ROWS,),
        in_specs=[row_spec, pl.BlockSpec((D_MODEL, D_MODEL), lambda i: (0, 0)), row_spec],
        out_specs=row_spec,
        out_shape=jax.ShapeDtypeStruct((n_rows, D_MODEL), F32),
        compiler_params=pltpu.CompilerParams(
            dimension_semantics=("arbitrary",), vmem_limit_bytes=VMEM_LIMIT),
        name="out_proj",
    )(og2d, w_bf16, x2d)


def _stack_halves_t(q):
    qt = q.astype(F32).T.astype(BF16)
    row = lax.broadcasted_iota(jnp.int32, qt.shape, 0)
    zero = jnp.zeros_like(qt)
    return jnp.concatenate(
        [jnp.where(row < HEAD_DIM, qt, zero), jnp.where(row >= HEAD_DIM, qt, zero)], axis=1)


def _silu(g):
    return g * (1.0 / (1.0 + jnp.exp(-g)))


def _diag_positions(tk, tq, key_off=0):
    c = lax.broadcasted_iota(jnp.int32, (tk, 2 * tq), 0) + key_off
    r = lax.broadcasted_iota(jnp.int32, (tk, 2 * tq), 1) & (tq - 1)
    return c, r


def _diff_attn_kernel(q_ref, k_ref, vt_ref, g_ref, sub_ref, lq1_ref, lk1_ref, lq2_ref,
                      lk2_ref, o_ref, acc_scr, s_scr, *, lambda_init):
    tq, tk = DIFF_Q_BLOCK, KEY_BLOCK
    i = pl.program_id(2)
    heads = range(ATT_HEADS)
    cols = [slice(h * LANES, (h + 1) * LANES) for h in heads]
    q2t = [_stack_halves_t(q_ref[0, :, cols[h]]) for h in heads]
    acc_scr[...] = jnp.zeros_like(acc_scr)

    def scores(j, h):
        start = pl.multiple_of(j * tk, tk)
        return jnp.dot(k_ref[0, pl.ds(start, tk), cols[h]], q2t[h],
                       preferred_element_type=F32)

    def step(j, state, diag=None, last=False):
        start = pl.multiple_of(j * tk, tk)
        new_state = []
        for h in heads:
            m_prev, l_prev = state[h]
            s = s_scr[h]
            if diag is not None:
                c, r = _diag_positions(tk, tq, diag * tk)
                s = jnp.where(c <= r, s, -jnp.inf)
            m_new = jnp.maximum(m_prev, jnp.max(s, axis=0, keepdims=True))
            alpha = jnp.exp2(m_prev - m_new)
            p = jnp.exp2(s - m_new).astype(BF16)
            if not last:
                s_scr[h] = scores(j + 1, h)
            pv = jnp.dot(vt_ref[0, h, :, pl.ds(start, tk)], p,
                         preferred_element_type=F32)
            acc_scr[h] = alpha * acc_scr[h] + pv[:LANES]
            new_state.append((m_new, alpha * l_prev + pv[LANES:LANES + 1]))
        return tuple(new_state)

    for h in heads:
        s_scr[h] = scores(0, h)
    m0 = jnp.full((1, 2 * tq), -jnp.inf, F32)
    l0 = jnp.zeros((1, 2 * tq), F32)
    n_diag = tq // tk
    n_left = i * n_diag

    def trip(jj, st):
        for u in range(DIFF_UNROLL):
            st = step(DIFF_UNROLL * jj + u, st)
        return st

    n_trips = n_left // DIFF_UNROLL
    state = lax.fori_loop(0, n_trips, trip, ((m0, l0),) * ATT_HEADS)

    def finish(st):
        lam = (jnp.exp(jnp.sum(lq1_ref[...] * lk1_ref[...], axis=-1, keepdims=True))
               - jnp.exp(jnp.sum(lq2_ref[...] * lk2_ref[...], axis=-1, keepdims=True))
               + lambda_init)
        for h in heads:
            acc = acc_scr[h]
            l = st[h][1]
            o = acc[:, :tq] / l[:, :tq] - lam * (acc[:, tq:] / l[:, tq:])
            ms = jnp.mean(o * o, axis=0, keepdims=True)
            on = (o * lax.rsqrt(ms + RMS_EPS)).T
            o_ref[0, :, cols[h]] = (on * sub_ref[...] * (1.0 - lambda_init)
                                    * _silu(g_ref[0, :, cols[h]])).astype(BF16)

    def tail(rest):
        def run(st):
            for u in range(rest):
                st = step(n_trips * DIFF_UNROLL + u, st)
            for d in range(n_diag):
                st = step(i * n_diag + d, st, diag=d, last=d == n_diag - 1)
            finish(st)
        return run

    lax.switch(n_left - n_trips * DIFF_UNROLL,
               [tail(rest) for rest in range(DIFF_UNROLL)], state)


def _diff_attention(q, k, vt, g, subln, lq1, lk1, lq2, lk2, lambda_init):
    b, s, _ = q.shape
    t = DIFF_Q_BLOCK
    w = ATT_HEADS * LANES
    q_spec = pl.BlockSpec((1, t, w), lambda bi, h, i: (bi, i, h))
    k_spec = pl.BlockSpec((1, s, w), lambda bi, h, i: (bi, 0, h),
                          pipeline_mode=pl.Buffered(1))
    vt_spec = pl.BlockSpec((1, ATT_HEADS, LANES + ONES_ROWS, s), lambda bi, h, i: (bi, h, 0, 0),
                           pipeline_mode=pl.Buffered(1))
    vec = lambda n: pl.BlockSpec((1, n), lambda bi, h, i: (0, 0))
    return pl.pallas_call(
        functools.partial(_diff_attn_kernel, lambda_init=lambda_init),
        grid=(b, N_COL_BLOCKS // ATT_HEADS, s // t),
        in_specs=[q_spec, k_spec, vt_spec, q_spec, vec(LANES),
                  vec(HEAD_DIM), vec(HEAD_DIM), vec(HEAD_DIM), vec(HEAD_DIM)],
        out_specs=q_spec,
        out_shape=jax.ShapeDtypeStruct((b, s, D_MODEL), BF16),
        scratch_shapes=[pltpu.VMEM((ATT_HEADS, LANES, 2 * t), F32),
                        pltpu.VMEM((ATT_HEADS, KEY_BLOCK, 2 * t), F32)],
        compiler_params=pltpu.CompilerParams(
            dimension_semantics=("arbitrary", "arbitrary", "arbitrary"),
            vmem_limit_bytes=VMEM_LIMIT),
        name="diff_attention",
    )(q, k, vt, g, subln.reshape(1, LANES), lq1.reshape(1, HEAD_DIM), lk1.reshape(1, HEAD_DIM),
      lq2.reshape(1, HEAD_DIM), lk2.reshape(1, HEAD_DIM))


def _sb_attn_kernel(q_ref, k_ref, vt_ref, g_ref, tri_ref, o_ref, acc_scr):
    t = KEY_BLOCK
    i = pl.program_id(2)
    pairs = range(SB_PAIRS)
    cols = [slice(h * LANES, (h + 1) * LANES) for h in pairs]
    q2t = [_stack_halves_t(q_ref[0, :, cols[h]]) for h in pairs]
    acc_scr[...] = jnp.zeros_like(acc_scr)

    def step(j, carry, masked):
        start = pl.multiple_of(j * t, t)
        tri = tri_ref[...]
        if masked:
            c, r = _diag_positions(t, t)
            keep = c < r
        zs, sps, log_betas, excls = {}, {}, {}, {}
        new_carry = [None] * SB_PAIRS

        def stage_a(h):
            zs[h] = jnp.dot(k_ref[0, pl.ds(start, t), cols[h]], q2t[h],
                            preferred_element_type=F32)

        def stage_b(h):
            z = zs.pop(h)
            sp = jnp.maximum(z, jnp.log2(1.0 + jnp.exp2(jnp.minimum(z, SOFTPLUS_CLAMP))))
            log_betas[h] = z - sp
            if masked:
                sp = jnp.where(keep, sp, 0.0)
            sps[h] = sp.astype(BF16)
            excls[h] = jnp.dot(tri, sps[h], preferred_element_type=F32)

        def stage_c(h):
            a = jnp.exp2(log_betas.pop(h) - excls[h] - carry[h])
            if masked:
                a = jnp.where(keep, a, 0.0)
            a = a.astype(BF16)
            vt = vt_ref[0, h, :, pl.ds(start, t)]
            acc_scr[h] += jnp.concatenate(
                [jnp.dot(vt[:HEAD_DIM], a[:, :t], preferred_element_type=F32),
                 jnp.dot(vt[HEAD_DIM:], a[:, t:], preferred_element_type=F32)], axis=0)
            new_carry[h] = carry[h] + excls.pop(h)[0:1, :] + sps.pop(h)[0:1, :].astype(F32)

        for n in range(SB_PAIRS + 2):
            if n < SB_PAIRS:
                stage_a(n)
            if 0 <= n - 1 < SB_PAIRS:
                stage_b(n - 1)
            if 0 <= n - 2 < SB_PAIRS:
                stage_c(n - 2)
        return tuple(new_carry)

    def smallest(carry):
        return jnp.min(functools.reduce(jnp.minimum, carry))

    zero = jnp.zeros((1, 2 * t), F32)
    carry = step(i, (zero,) * SB_PAIRS, True)

    def keep_walking(st):
        jj, _, low = st
        return jnp.logical_and(jj < i, low < SB_DEAD_CARRY)

    def walk(st):
        jj, c, _ = st
        c = step(i - 1 - jj, c, False)
        return jj + 1, c, smallest(c)

    lax.while_loop(keep_walking, walk, (jnp.int32(0), carry, smallest(carry)))

    for h in pairs:
        o = acc_scr[h].T
        o_ref[0, :, cols[h]] = (o * _silu(g_ref[0, :, cols[h]])).astype(BF16)


def _sb_attention(q, k, vt, g, tri):
    b, s, _ = q.shape
    t = KEY_BLOCK
    w = SB_PAIRS * LANES
    q_spec = pl.BlockSpec((1, t, w), lambda bi, h, i: (bi, i, h))
    k_spec = pl.BlockSpec((1, s, w), lambda bi, h, i: (bi, 0, h),
                          pipeline_mode=pl.Buffered(1))
    vt_spec = pl.BlockSpec((1, SB_PAIRS, LANES, s), lambda bi, h, i: (bi, h, 0, 0),
                           pipeline_mode=pl.Buffered(1))
    return pl.pallas_call(
        _sb_attn_kernel,
        grid=(b, N_COL_BLOCKS // SB_PAIRS, s // t),
        in_specs=[q_spec, k_spec, vt_spec, q_spec,
                  pl.BlockSpec((t, t), lambda bi, h, i: (0, 0))],
        out_specs=q_spec,
        out_shape=jax.ShapeDtypeStruct((b, s, D_MODEL), BF16),
        scratch_shapes=[pltpu.VMEM((SB_PAIRS, LANES, t), F32)],
        compiler_params=pltpu.CompilerParams(
            dimension_semantics=("arbitrary", "arbitrary", "arbitrary"),
            vmem_limit_bytes=VMEM_LIMIT),
        name="sb_attention",
    )(q, k, vt, g, tri)


def _rope_tables(seq_len):
    half = ROT_DIM // 2
    lane = jnp.arange(LANES) % HEAD_DIM
    inv = jnp.where(lane < ROT_DIM,
                    jnp.power(jnp.float32(ROPE_THETA), -(lane % half).astype(F32) / half), 0.0)
    ang = jnp.arange(seq_len).astype(F32)[:, None] * inv[None, :]
    cos, sin = jnp.cos(ang), jnp.sin(ang)
    return (cos, jnp.where(lane < half, -sin, 0.0),
            jnp.where((lane >= half) & (lane < ROT_DIM), sin, 0.0))


def kernel(x, a_norm, a_w_in, a_w_out, a_q_norm, a_k_norm, a_lq1, a_lk1, a_lq2, a_lk2,
           a_subln, kv_norm, w_kv, b_norm, b_w_in, b_w_out):
    b, s, d = x.shape
    assert d == D_MODEL and s % PROJ_ROWS == 0 and s % DIFF_Q_BLOCK == 0
    assert DIFF_Q_BLOCK % KEY_BLOCK == 0
    n = b * s
    shp = (b, s, d)
    x2 = x.reshape(n, d)

    sub = jnp.arange(PROJ_COLS) // HEAD_DIM
    block_diag = (sub[:, None] == sub[None, :]).astype(BF16)
    idx = jnp.arange(KEY_BLOCK)
    tri = (idx[None, :] > idx[:, None]).astype(BF16)
    rope = (block_diag,) + _rope_tables(s)
    tile2 = lambda v: jnp.concatenate([v, v]).reshape(1, LANES)

    pending = None
    for layer in range(N_A_LAYERS):
        outs = _in_proj(
            x2, a_norm[layer][None], a_w_in[layer].astype(BF16), ("qk", "qk", "vt1", "f32"), s,
            rope=rope,
            head_gains=(tile2(a_q_norm[layer]) * Q_SCALE, tile2(a_k_norm[layer])),
            residual=pending)
        if pending is not None:
            x2 = outs.pop()
        q, k, vt, g = outs
        og = _diff_attention(q.reshape(shp), k.reshape(shp), vt, g.reshape(shp),
                             a_subln[layer], a_lq1[layer], a_lk1[layer], a_lq2[layer],
                             a_lk2[layer], LAMBDA_INIT[layer])
        pending = (og.reshape(n, d), a_w_out[layer].astype(BF16))

    for j in range(N_B_LAYERS):
        if j == 0:
            outs = _in_proj(
                x2, jnp.stack([kv_norm, b_norm[0]]),
                jnp.concatenate([w_kv.astype(BF16), b_w_in[0].astype(BF16)], axis=1),
                ("bf16", "vt", "q_scaled", "f32"), s, gain_ids=(0, 0, 1, 1), residual=pending)
            x2 = outs.pop()
            sk, svt, q, g = outs
        else:
            q, g, x2 = _in_proj(x2, b_norm[j][None], b_w_in[j].astype(BF16),
                                ("q_scaled", "f32"), s, residual=pending)
        og = _sb_attention(q.reshape(shp), sk.reshape(shp), svt, g.reshape(shp), tri)
        pending = (og.reshape(n, d), b_w_out[j].astype(BF16))
    return _out_proj(*pending, x2).reshape(b, s, d)
```

```python
import functools
import math

import jax
import jax.numpy as jnp
from jax import lax
from jax.experimental import pallas as pl
from jax.experimental.pallas import tpu as pltpu

D_MODEL = 1024
HEAD_DIM = 64
LANES = 128
N_COL_BLOCKS = D_MODEL // LANES
N_A_LAYERS = 2
N_B_LAYERS = 2
LAMBDA_INIT = tuple(0.8 - 0.6 * math.exp(-0.3 * l) for l in range(N_A_LAYERS))
ROPE_THETA = 500000.0
ROT_DIM = HEAD_DIM // 4
RMS_EPS = 1e-6
LOG2E = 1.4426950408889634
SOFTPLUS_CLAMP = 64.0
SB_DEAD_CARRY = 160.0
Q_SCALE = HEAD_DIM ** -0.5 * LOG2E

PROJ_ROWS = 512
OUT_PROJ_ROWS = 1024
PROJ_COLS = 256
KEY_BLOCK = 256
DIFF_Q_BLOCK = 256
ATT_HEADS = 8
DIFF_UNROLL = 4
ONES_ROWS = 16
SB_PAIRS = 8
VMEM_LIMIT = 56 * 1024 * 1024

F32 = jnp.float32
BF16 = jnp.bfloat16


def _in_proj_kernel(*refs, kinds, gain_ids, residual):
    n_qk = sum(k == "qk" for k in kinds)
    x_ref, gain_ref, w_ref = refs[:3]
    pos = 3
    if residual:
        og_ref, w_out_ref = refs[pos:pos + 2]
        pos += 2
    if n_qk:
        bd_ref, cos_ref, sa_ref, sb_ref = refs[pos:pos + 4]
        pos += 4
        head_gain_refs = refs[pos:pos + n_qk]
        pos += n_qk
    out_refs = refs[pos:]

    x = x_ref[...]
    if residual:
        x = x + jnp.dot(og_ref[...], w_out_ref[...], preferred_element_type=F32)
        out_refs[len(kinds)][...] = x
    ms = jnp.mean(x * x, axis=-1, keepdims=True)
    xr = x * lax.rsqrt(ms + RMS_EPS)
    xn = {g: (xr * gain_ref[g:g + 1, :]).astype(BF16) for g in sorted(set(gain_ids))}

    head_gains = {}
    for gi, kind in enumerate(kinds):
        if kind == "qk":
            head_gains[gi] = head_gain_refs[len(head_gains)][...]

    def finish(t, gi, c):
        kind, out_ref = kinds[gi], out_refs[gi]
        if kind == "qk":
            ss = jnp.dot((t * t).astype(BF16), bd_ref[...], preferred_element_type=F32)
            tn = t * lax.rsqrt(ss * (1.0 / HEAD_DIM) + RMS_EPS)
            for h in range(PROJ_COLS // LANES):
                th = tn[:, h * LANES:(h + 1) * LANES] * head_gains[gi]
                rot = (th * cos_ref[...]
                       + pltpu.roll(th, LANES - ROT_DIM // 2, 1) * sa_ref[...]
                       + pltpu.roll(th, ROT_DIM // 2, 1) * sb_ref[...])
                lo = c + h * LANES
                out_ref[:, lo:lo + LANES] = rot.astype(BF16)
        elif kind in ("vt", "vt1"):
            for h in range(PROJ_COLS // LANES):
                blk = (c + h * LANES) // LANES
                out_ref[0, blk, 0:LANES, :] = t[:, h * LANES:(h + 1) * LANES].T.astype(BF16)
                if kind == "vt1":
                    out_ref[0, blk, LANES:LANES + ONES_ROWS, :] = jnp.ones(
                        (ONES_ROWS, t.shape[0]), BF16)
        elif kind == "bf16":
            out_ref[:, c:c + PROJ_COLS] = t.astype(BF16)
        elif kind == "q_scaled":
            out_ref[:, c:c + PROJ_COLS] = (t * Q_SCALE).astype(BF16)
        else:
            out_ref[:, c:c + PROJ_COLS] = t

    def project(gi, c):
        col = gi * D_MODEL + c
        return jnp.dot(xn[gain_ids[gi]], w_ref[:, col:col + PROJ_COLS],
                       preferred_element_type=F32)

    chunks = [(gi, c) for gi in range(len(kinds)) for c in range(0, D_MODEL, PROJ_COLS)]
    t_next = project(*chunks[0])
    for n, (gi, c) in enumerate(chunks):
        t = t_next
        if n + 1 < len(chunks):
            t_next = project(*chunks[n + 1])
        finish(t, gi, c)


def _in_proj(x2d, gains, w_bf16, kinds, seq_len, rope=None, head_gains=(), gain_ids=None,
             residual=None):
    n_rows = x2d.shape[0]
    n_out = len(kinds)
    gain_ids = tuple(gain_ids) if gain_ids is not None else (0,) * n_out
    assert gains.shape == (max(gain_ids) + 1, D_MODEL) and len(gain_ids) == n_out
    assert w_bf16.shape == (D_MODEL, n_out * D_MODEL)
    blocks_per_seq = seq_len // PROJ_ROWS
    row_spec = pl.BlockSpec((PROJ_ROWS, D_MODEL), lambda i: (i, 0))
    const2 = lambda i: (0, 0)
    resident = lambda shape: pl.BlockSpec(shape, const2, pipeline_mode=pl.Buffered(1))
    in_specs = [row_spec,
                pl.BlockSpec(gains.shape, const2),
                resident((D_MODEL, n_out * D_MODEL))]
    args = [x2d, gains, w_bf16]
    if residual is not None:
        in_specs += [row_spec, resident((D_MODEL, D_MODEL))]
        args += list(residual)
    if rope is not None:
        tab_spec = pl.BlockSpec((PROJ_ROWS, LANES), lambda i: (i % blocks_per_seq, 0))
        in_specs += [pl.BlockSpec((PROJ_COLS, PROJ_COLS), const2), tab_spec, tab_spec, tab_spec]
        args += list(rope)
        for hg in head_gains:
            in_specs.append(pl.BlockSpec((1, LANES), const2))
            args.append(hg)
    out_specs, out_shape = [], []
    for k in kinds:
        if k in ("vt", "vt1"):
            rows = LANES + (ONES_ROWS if k == "vt1" else 0)
            out_specs.append(pl.BlockSpec(
                (1, N_COL_BLOCKS, rows, PROJ_ROWS),
                lambda i: (i // blocks_per_seq, 0, 0, i % blocks_per_seq)))
            out_shape.append(jax.ShapeDtypeStruct(
                (n_rows // seq_len, N_COL_BLOCKS, rows, seq_len), BF16))
        else:
            out_specs.append(row_spec)
            out_shape.append(jax.ShapeDtypeStruct((n_rows, D_MODEL), F32 if k == "f32" else BF16))
    if residual is not None:
        out_specs.append(row_spec)
        out_shape.append(jax.ShapeDtypeStruct((n_rows, D_MODEL), F32))
    return list(pl.pallas_call(
        functools.partial(_in_proj_kernel, kinds=tuple(kinds), gain_ids=gain_ids,
                          residual=residual is not None),
        grid=(n_rows // PROJ_ROWS,),
        in_specs=in_specs,
        out_specs=out_specs,
        out_shape=out_shape,
        compiler_params=pltpu.CompilerParams(
            dimension_semantics=("arbitrary",), vmem_limit_bytes=VMEM_LIMIT),
        name=("res_" if residual is not None else "") + "in_proj_" + "_".join(kinds),
    )(*args))


def _out_proj_kernel(og_ref, w_ref, x_ref, o_ref):
    o_ref[...] = x_ref[...] + jnp.dot(og_ref[...], w_ref[...], preferred_element_type=F32)


def _out_proj(og2d, w_bf16, x2d):
    n_rows = x2d.shape[0]
    row_spec = pl.BlockSpec((OUT_PROJ_ROWS, D_MODEL), lambda i: (i, 0))
    return pl.pallas_call(
        _out_proj_kernel,
        grid=(n_rows // OUT_PROJ_ROWS,),
        in_specs=[row_spec, pl.BlockSpec((D_MODEL, D_MODEL), lambda i: (0, 0)), row_spec],
        out_specs=row_spec,
        out_shape=jax.ShapeDtypeStruct((n_rows, D_MODEL), F32),
        compiler_params=pltpu.CompilerParams(
            dimension_semantics=("arbitrary",), vmem_limit_bytes=VMEM_LIMIT),
        name="out_proj",
    )(og2d, w_bf16, x2d)


def _stack_halves_t(q):
    qt = q.astype(F32).T.astype(BF16)
    row = lax.broadcasted_iota(jnp.int32, qt.shape, 0)
    zero = jnp.zeros_like(qt)
    return jnp.concatenate(
        [jnp.where(row < HEAD_DIM, qt, zero), jnp.where(row >= HEAD_DIM, qt, zero)], axis=1)


def _silu(g):
    return g * (1.0 / (1.0 + jnp.exp(-g)))


def _diag_positions(tk, tq, key_off=0):
    c = lax.broadcasted_iota(jnp.int32, (tk, 2 * tq), 0) + key_off
    r = lax.broadcasted_iota(jnp.int32, (tk, 2 * tq), 1) & (tq - 1)
    return c, r


def _diff_attn_kernel(q_ref, k_ref, vt_ref, g_ref, sub_ref, lq1_ref, lk1_ref, lq2_ref,
                      lk2_ref, o_ref, acc_scr, s_scr, *, lambda_init):
    tq, tk = DIFF_Q_BLOCK, KEY_BLOCK
    i = pl.program_id(2)
    heads = range(ATT_HEADS)
    cols = [slice(h * LANES, (h + 1) * LANES) for h in heads]
    q2t = [_stack_halves_t(q_ref[0, :, cols[h]]) for h in heads]
    acc_scr[...] = jnp.zeros_like(acc_scr)

    def scores(j, h):
        start = pl.multiple_of(j * tk, tk)
        return jnp.dot(k_ref[0, pl.ds(start, tk), cols[h]], q2t[h],
                       preferred_element_type=F32)

    def step(j, state, diag=None, last=False):
        start = pl.multiple_of(j * tk, tk)
        new_state = []
        for h in heads:
            m_prev, l_prev = state[h]
            s = s_scr[h]
            if diag is not None:
                c, r = _diag_positions(tk, tq, diag * tk)
                s = jnp.where(c <= r, s, -jnp.inf)
            m_new = jnp.maximum(m_prev, jnp.max(s, axis=0, keepdims=True))
            alpha = jnp.exp2(m_prev - m_new)
            p = jnp.exp2(s - m_new).astype(BF16)
            if not last:
                s_scr[h] = scores(j + 1, h)
            pv = jnp.dot(vt_ref[0, h, :, pl.ds(start, tk)], p,
                         preferred_element_type=F32)
            acc_scr[h] = alpha * acc_scr[h] + pv[:LANES]
            new_state.append((m_new, alpha * l_prev + pv[LANES:LANES + 1]))
        return tuple(new_state)

    for h in heads:
        s_scr[h] = scores(0, h)
    m0 = jnp.full((1, 2 * tq), -jnp.inf, F32)
    l0 = jnp.zeros((1, 2 * tq), F32)
    n_diag = tq // tk
    n_left = i * n_diag

    def trip(jj, st):
        for u in range(DIFF_UNROLL):
            st = step(DIFF_UNROLL * jj + u, st)
        return st

    n_trips = n_left // DIFF_UNROLL
    state = lax.fori_loop(0, n_trips, trip, ((m0, l0),) * ATT_HEADS)

    def finish(st):
        lam = (jnp.exp(jnp.sum(lq1_ref[...] * lk1_ref[...], axis=-1, keepdims=True))
               - jnp.exp(jnp.sum(lq2_ref[...] * lk2_ref[...], axis=-1, keepdims=True))
               + lambda_init)
        for h in heads:
            acc = acc_scr[h]
            l = st[h][1]
            o = acc[:, :tq] / l[:, :tq] - lam * (acc[:, tq:] / l[:, tq:])
            ms = jnp.mean(o * o, axis=0, keepdims=True)
            on = (o * lax.rsqrt(ms + RMS_EPS)).T
            o_ref[0, :, cols[h]] = (on * sub_ref[...] * (1.0 - lambda_init)
                                    * _silu(g_ref[0, :, cols[h]])).astype(BF16)

    def tail(rest):
        def run(st):
            for u in range(rest):
                st = step(n_trips * DIFF_UNROLL + u, st)
            for d in range(n_diag):
                st = step(i * n_diag + d, st, diag=d, last=d == n_diag - 1)
            finish(st)
        return run

    lax.switch(n_left - n_trips * DIFF_UNROLL,
               [tail(rest) for rest in range(DIFF_UNROLL)], state)


def _diff_attention(q, k, vt, g, subln, lq1, lk1, lq2, lk2, lambda_init):
    b, s, _ = q.shape
    t = DIFF_Q_BLOCK
    w = ATT_HEADS * LANES
    q_spec = pl.BlockSpec((1, t, w), lambda bi, h, i: (bi, i, h))
    k_spec = pl.BlockSpec((1, s, w), lambda bi, h, i: (bi, 0, h),
                          pipeline_mode=pl.Buffered(1))
    vt_spec = pl.BlockSpec((1, ATT_HEADS, LANES + ONES_ROWS, s), lambda bi, h, i: (bi, h, 0, 0),
                           pipeline_mode=pl.Buffered(1))
    vec = lambda n: pl.BlockSpec((1, n), lambda bi, h, i: (0, 0))
    return pl.pallas_call(
        functools.partial(_diff_attn_kernel, lambda_init=lambda_init),
        grid=(b, N_COL_BLOCKS // ATT_HEADS, s // t),
        in_specs=[q_spec, k_spec, vt_spec, q_spec, vec(LANES),
                  vec(HEAD_DIM), vec(HEAD_DIM), vec(HEAD_DIM), vec(HEAD_DIM)],
        out_specs=q_spec,
        out_shape=jax.ShapeDtypeStruct((b, s, D_MODEL), BF16),
        scratch_shapes=[pltpu.VMEM((ATT_HEADS, LANES, 2 * t), F32),
                        pltpu.VMEM((ATT_HEADS, KEY_BLOCK, 2 * t), F32)],
        compiler_params=pltpu.CompilerParams(
            dimension_semantics=("arbitrary", "arbitrary", "arbitrary"),
            vmem_limit_bytes=VMEM_LIMIT),
        name="diff_attention",
    )(q, k, vt, g, subln.reshape(1, LANES), lq1.reshape(1, HEAD_DIM), lk1.reshape(1, HEAD_DIM),
      lq2.reshape(1, HEAD_DIM), lk2.reshape(1, HEAD_DIM))


def _sb_attn_kernel(q_ref, k_ref, vt_ref, g_ref, tri_ref, o_ref, acc_scr):
    t = KEY_BLOCK
    i = pl.program_id(2)
    pairs = range(SB_PAIRS)
    cols = [slice(h * LANES, (h + 1) * LANES) for h in pairs]
    q2t = [_stack_halves_t(q_ref[0, :, cols[h]]) for h in pairs]
    acc_scr[...] = jnp.zeros_like(acc_scr)

    def step(j, carry, masked):
        start = pl.multiple_of(j * t, t)
        tri = tri_ref[...]
        if masked:
            c, r = _diag_positions(t, t)
            keep = c < r
        zs, sps, log_betas, excls = {}, {}, {}, {}
        new_carry = [None] * SB_PAIRS

        def stage_a(h):
            zs[h] = jnp.dot(k_ref[0, pl.ds(start, t), cols[h]], q2t[h],
                            preferred_element_type=F32)

        def stage_b(h):
            z = zs.pop(h)
            sp = jnp.maximum(z, jnp.log2(1.0 + jnp.exp2(jnp.minimum(z, SOFTPLUS_CLAMP))))
            log_betas[h] = z - sp
            if masked:
                sp = jnp.where(keep, sp, 0.0)
            sps[h] = sp.astype(BF16)
            excls[h] = jnp.dot(tri, sps[h], preferred_element_type=F32)

        def stage_c(h):
            a = jnp.exp2(log_betas.pop(h) - excls[h] - carry[h])
            if masked:
                a = jnp.where(keep, a, 0.0)
            a = a.astype(BF16)
            vt = vt_ref[0, h, :, pl.ds(start, t)]
            acc_scr[h] += jnp.concatenate(
                [jnp.dot(vt[:HEAD_DIM], a[:, :t], preferred_element_type=F32),
                 jnp.dot(vt[HEAD_DIM:], a[:, t:], preferred_element_type=F32)], axis=0)
            new_carry[h] = carry[h] + excls.pop(h)[0:1, :] + sps.pop(h)[0:1, :].astype(F32)

        for n in range(SB_PAIRS + 2):
            if n < SB_PAIRS:
                stage_a(n)
            if 0 <= n - 1 < SB_PAIRS:
                stage_b(n - 1)
            if 0 <= n - 2 < SB_PAIRS:
                stage_c(n - 2)
        return tuple(new_carry)

    def smallest(carry):
        return jnp.min(functools.reduce(jnp.minimum, carry))

    zero = jnp.zeros((1, 2 * t), F32)
    carry = step(i, (zero,) * SB_PAIRS, True)

    def keep_walking(st):
        jj, _, low = st
        return jnp.logical_and(jj < i, low < SB_DEAD_CARRY)

    def walk(st):
        jj, c, _ = st
        c = step(i - 1 - jj, c, False)
        return jj + 1, c, smallest(c)

    lax.while_loop(keep_walking, walk, (jnp.int32(0), carry, smallest(carry)))

    for h in pairs:
        o = acc_scr[h].T
        o_ref[0, :, cols[h]] = (o * _silu(g_ref[0, :, cols[h]])).astype(BF16)


def _sb_attention(q, k, vt, g, tri):
    b, s, _ = q.shape
    t = KEY_BLOCK
    w = SB_PAIRS * LANES
    q_spec = pl.BlockSpec((1, t, w), lambda bi, h, i: (bi, i, h))
    k_spec = pl.BlockSpec((1, s, w), lambda bi, h, i: (bi, 0, h),
                          pipeline_mode=pl.Buffered(1))
    vt_spec = pl.BlockSpec((1, SB_PAIRS, LANES, s), lambda bi, h, i: (bi, h, 0, 0),
                           pipeline_mode=pl.Buffered(1))
    return pl.pallas_call(
        _sb_attn_kernel,
        grid=(b, N_COL_BLOCKS // SB_PAIRS, s // t),
        in_specs=[q_spec, k_spec, vt_spec, q_spec,
                  pl.BlockSpec((t, t), lambda bi, h, i: (0, 0))],
        out_specs=q_spec,
        out_shape=jax.ShapeDtypeStruct((b, s, D_MODEL), BF16),
        scratch_shapes=[pltpu.VMEM((SB_PAIRS, LANES, t), F32)],
        compiler_params=pltpu.CompilerParams(
            dimension_semantics=("arbitrary", "arbitrary", "arbitrary"),
            vmem_limit_bytes=VMEM_LIMIT),
        name="sb_attention",
    )(q, k, vt, g, tri)


def _rope_tables(seq_len):
    half = ROT_DIM // 2
    lane = jnp.arange(LANES) % HEAD_DIM
    inv = jnp.where(lane < ROT_DIM,
                    jnp.power(jnp.float32(ROPE_THETA), -(lane % half).astype(F32) / half), 0.0)
    ang = jnp.arange(seq_len).astype(F32)[:, None] * inv[None, :]
    cos, sin = jnp.cos(ang), jnp.sin(ang)
    return (cos, jnp.where(lane < half, -sin, 0.0),
            jnp.where((lane >= half) & (lane < ROT_DIM), sin, 0.0))


def kernel(x, a_norm, a_w_in, a_w_out, a_q_norm, a_k_norm, a_lq1, a_lk1, a_lq2, a_lk2,
           a_subln, kv_norm, w_kv, b_norm, b_w_in, b_w_out):
    b, s, d = x.shape
    assert d == D_MODEL and s % PROJ_ROWS == 0 and s % DIFF_Q_BLOCK == 0
    assert DIFF_Q_BLOCK % KEY_BLOCK == 0
    n = b * s
    shp = (b, s, d)
    x2 = x.reshape(n, d)

    sub = jnp.arange(PROJ_COLS) // HEAD_DIM
    block_diag = (sub[:, None] == sub[None, :]).astype(BF16)
    idx = jnp.arange(KEY_BLOCK)
    tri = (idx[None, :] > idx[:, None]).astype(BF16)
    rope = (block_diag,) + _rope_tables(s)
    tile2 = lambda v: jnp.concatenate([v, v]).reshape(1, LANES)

    pending = None
    for layer in range(N_A_LAYERS):
        outs = _in_proj(
            x2, a_norm[layer][None], a_w_in[layer].astype(BF16), ("qk", "qk", "vt1", "f32"), s,
            rope=rope,
            head_gains=(tile2(a_q_norm[layer]) * Q_SCALE, tile2(a_k_norm[layer])),
            residual=pending)
        if pending is not None:
            x2 = outs.pop()
        q, k, vt, g = outs
        og = _diff_attention(q.reshape(shp), k.reshape(shp), vt, g.reshape(shp),
                             a_subln[layer], a_lq1[layer], a_lk1[layer], a_lq2[layer],
                             a_lk2[layer], LAMBDA_INIT[layer])
        pending = (og.reshape(n, d), a_w_out[layer].astype(BF16))

    for j in range(N_B_LAYERS):
        if j == 0:
            outs = _in_proj(
                x2, jnp.stack([kv_norm, b_norm[0]]),
                jnp.concatenate([w_kv.astype(BF16), b_w_in[0].astype(BF16)], axis=1),
                ("bf16", "vt", "q_scaled", "f32"), s, gain_ids=(0, 0, 1, 1), residual=pending)
            x2 = outs.pop()
            sk, svt, q, g = outs
        else:
            q, g, x2 = _in_proj(x2, b_norm[j][None], b_w_in[j].astype(BF16),
                                ("q_scaled", "f32"), s, residual=pending)
        og = _sb_attention(q.reshape(shp), sk.reshape(shp), svt, g.reshape(shp), tri)
        pending = (og.reshape(n, d), b_w_out[j].astype(BF16))
    return _out_proj(*pending, x2).reshape(b, s, d)
```
